```python
import math
import jax, jax.numpy as jnp
from jax import lax
import numpy as np

D_MODEL = 1024
BATCH = 2
SEQ = 8192
DEPTH = 4

N_MIXERS = 3
N_CONV_LAYERS = (DEPTH + 2) // 3
N_DIL_LAYERS = (DEPTH + 1) // 3
N_RWKV_LAYERS = DEPTH // 3

CONV_WIDTH = 31
DIL_GROUPS = ((128, 1), (512, 4), (2048, 16))
DIL_HEADS = 16
DIL_HEAD_DIM = D_MODEL // DIL_HEADS
REL_BUCKETS = 32
REL_MAX_DIST = 2048
RWKV_HEAD = 64
RWKV_HEADS = D_MODEL // RWKV_HEAD
DECAY_LORA = 64
ICLR_LORA = 64
GATE_LORA = 128
RWKV_GN_EPS = 64e-5
MEM_LEN = 256
XATTN_HEADS = 4
XATTN_HEAD_DIM = D_MODEL // XATTN_HEADS
D_FF = 4 * D_MODEL
DEEPNORM_ALPHA = (2 * DEPTH) ** 0.25
DEEPNORM_BETA = (8 * DEPTH) ** -0.25
LN_EPS = 1e-5

kernel_name = "hybrid_conv_dilattn_rwkv7_deepnorm"

F32 = jnp.float32


def layer_norm(x, g, b, eps=LN_EPS):
    xf = x.astype(F32)
    mu = jnp.mean(xf, axis=-1, keepdims=True)
    var = jnp.mean(jnp.square(xf - mu), axis=-1, keepdims=True)
    return ((xf - mu) * lax.rsqrt(var + eps) * g + b).astype(x.dtype)


def conv_module(x, w_in, b_in, dw, dw_b, ln_g, ln_b, w_out, b_out):
    h = x @ w_in + b_in
    val, gate = jnp.split(h, 2, axis=-1)
    h = val * jax.nn.sigmoid(gate)
    h = lax.conv_general_dilated(
        h, dw[:, None, :].astype(h.dtype), window_strides=(1,),
        padding=[(CONV_WIDTH - 1, 0)],
        dimension_numbers=('NWC', 'WIO', 'NWC'),
        feature_group_count=D_MODEL) + dw_b
    h = jax.nn.silu(layer_norm(h, ln_g, ln_b))
    return h @ w_out + b_out


def t5_causal_bucket(dist):
    n = jnp.maximum(dist, 0)
    max_exact = REL_BUCKETS // 2
    nf = jnp.maximum(n, 1).astype(F32)
    large = max_exact + (jnp.log(nf / max_exact) / math.log(REL_MAX_DIST / max_exact)
                         * (REL_BUCKETS - max_exact)).astype(jnp.int32)
    large = jnp.minimum(large, REL_BUCKETS - 1)
    return jnp.where(n < max_exact, n, large)


def dilated_group_attention(q, k, v, window, dilation, rel_bias):
    b, s, h, e = q.shape
    blk = window // dilation
    sub_len = s // dilation
    nb = -(-sub_len // blk)
    lp = nb * blk

    def to_blocks(t):
        t = t.reshape(b, sub_len, dilation, h, e).transpose(0, 2, 1, 3, 4)
        t = jnp.pad(t, ((0, 0), (0, 0), (0, lp - sub_len), (0, 0), (0, 0)))
        return t.reshape(b, dilation, nb, blk, h, e)

    def with_prev(t):
        prev = jnp.pad(t[:, :, :-1], ((0, 0), (0, 0), (1, 0), (0, 0), (0, 0), (0, 0)))
        return jnp.concatenate([prev, t], axis=3)

    qb = to_blocks(q)
    kw = with_prev(to_blocks(k))
    vw = with_prev(to_blocks(v))
    logits = jnp.einsum('bdnqhe,bdnkhe->bdnhqk', qb, kw,
                        preferred_element_type=F32) * (e ** -0.5)
    qi = jnp.arange(blk)[:, None]
    kj = jnp.arange(2 * blk)[None, :]
    rel = qi + blk - kj
    key_pos = jnp.arange(nb)[:, None, None] * blk + kj[None] - blk
    valid = ((rel >= 0) & (rel <= blk))[None] & (key_pos >= 0)
    bias = jnp.transpose(rel_bias[t5_causal_bucket(rel * dilation)], (2, 0, 1)).astype(F32)
    logits = jnp.where(valid[None, None, :, None], logits + bias[None, None, None], -jnp.inf)
    m = jnp.max(logits, axis=-1, keepdims=True)
    p = jnp.exp(logits - m)
    den = jnp.sum(p, axis=-1, keepdims=True)
    o = jnp.einsum('bdnhqk,bdnkhe->bdnqhe', p / den, vw.astype(F32))
    lse = (m + jnp.log(den))[..., 0]
    o = o.reshape(b, dilation, lp, h, e)[:, :, :sub_len].transpose(0, 2, 1, 3, 4).reshape(b, s, h, e)
    lse = jnp.swapaxes(lse, -1, -2).reshape(b, dilation, lp, h)[:, :, :sub_len]
    lse = lse.transpose(0, 2, 1, 3).reshape(b, s, h)
    return o, lse


def dilated_attention(x, w_qkv, w_out, rel_bias):
    b, s, _ = x.shape
    n_groups = len(DIL_GROUPS)
    qkv = (x @ w_qkv).reshape(b, s, n_groups, 3, DIL_HEADS, DIL_HEAD_DIM)
    outs, lses = [], []
    for g, (win, dil) in enumerate(DIL_GROUPS):
        o, lse = dilated_group_attention(qkv[:, :, g, 0], qkv[:, :, g, 1], qkv[:, :, g, 2],
                                         win, dil, rel_bias)
        outs.append(o)
        lses.append(lse)
    wts = jax.nn.softmax(jnp.stack(lses), axis=0)
    o = jnp.einsum('gbsh,gbshe->bshe', wts, jnp.stack(outs))
    return o.reshape(b, s, DIL_HEADS * DIL_HEAD_DIM).astype(x.dtype) @ w_out


def rwkv7_time_mix(x, mu, w_rkv, w0, w1, w2, a0, a1, a2, g1, g2, k_k, k_a, r_k,
                   lnx_g, lnx_b, w_out):
    b, t, c = x.shape
    h, n = RWKV_HEADS, RWKV_HEAD
    xx = jnp.pad(x, ((0, 0), (1, 0), (0, 0)))[:, :-1] - x
    xs = x[None] + xx[None] * mu[:, None, None, :]
    rkv = jnp.einsum('sbtc,scd->sbtd', xs[:3], w_rkv)
    r, k, v = rkv[0], rkv[1], rkv[2]
    xw, xa, xg = xs[3], xs[4], xs[5]
    w_log = -jax.nn.softplus(-(w0 + jnp.tanh(xw @ w1) @ w2)) - 0.5
    decay = jnp.exp(-jnp.exp(w_log.astype(F32)))
    a = jax.nn.sigmoid(a0 + (xa @ a1) @ a2)
    g = jax.nn.sigmoid(xg @ g1) @ g2

    def heads(z):
        return z.astype(F32).reshape(b, t, h, n)

    kk = heads(k * k_k)
    kk = kk / jnp.maximum(jnp.sqrt(jnp.sum(jnp.square(kk), axis=-1, keepdims=True)), 1e-12)
    k = k * (1 + (a - 1) * k_a)
    r_h, k_h, v_h, a_h = heads(r), heads(k), heads(v), heads(a)
    w_h = decay.reshape(b, t, h, n)

    def step(state, inp):
        r_t, w_t, k_t, v_t, kk_t, a_t = inp
        sa = jnp.einsum('bhvk,bhk->bhv', state, -kk_t)
        state = (state * w_t[:, :, None, :]
                 + sa[..., None] * (kk_t * a_t)[:, :, None, :]
                 + v_t[..., None] * k_t[:, :, None, :])
        return state, jnp.einsum('bhvk,bhk->bhv', state, r_t)

    def seq_first(z):
        return jnp.swapaxes(z, 0, 1)

    state0 = jnp.zeros((b, h, n, n), F32)
    _, y = lax.scan(step, state0, (seq_first(r_h), seq_first(w_h), seq_first(k_h),
                                   seq_first(v_h), seq_first(kk), seq_first(a_h)))
    y = jnp.swapaxes(y, 0, 1)
    y_mu = jnp.mean(y, axis=-1, keepdims=True)
    y_var = jnp.mean(jnp.square(y - y_mu), axis=-1, keepdims=True)
    y = ((y - y_mu) * lax.rsqrt(y_var + RWKV_GN_EPS)).reshape(b, t, c) * lnx_g + lnx_b
    bonus = jnp.sum(r_h * k_h * r_k.astype(F32), axis=-1, keepdims=True) * v_h
    y = y + bonus.reshape(b, t, c)
    return (y * g).astype(x.dtype) @ w_out


def memory_cross_attention(x, mem, w_q, w_kv, w_out):
    b, s, _ = x.shape
    q = (x @ w_q).reshape(b, s, XATTN_HEADS, XATTN_HEAD_DIM)
    kv = (mem @ w_kv).reshape(b, mem.shape[1], 2, XATTN_HEADS, XATTN_HEAD_DIM)
    logits = jnp.einsum('bshe,bmhe->bhsm', q, kv[:, :, 0],
                        preferred_element_type=F32) * (XATTN_HEAD_DIM ** -0.5)
    p = jax.nn.softmax(logits, axis=-1)
    o = jnp.einsum('bhsm,bmhe->bshe', p, kv[:, :, 1].astype(F32))
    return o.reshape(b, s, D_MODEL).astype(x.dtype) @ w_out


def sq_relu_mlp(x, w1, w2):
    return jnp.square(jax.nn.relu(x @ w1)) @ w2


def setup_inputs(seed: int = 0) -> dict:
    key = jax.random.key(seed)
    ks = iter(jax.random.split(key, 40))

    def normal(shape, scale):
        return jax.random.normal(next(ks), shape, F32) * scale

    D = D_MODEL
    nA, nB, nC = N_CONV_LAYERS, N_DIL_LAYERS, N_RWKV_LAYERS
    qkv_cols = len(DIL_GROUPS) * 3 * DIL_HEADS * DIL_HEAD_DIM
    inp = {}
    inp["x"] = normal((BATCH, SEQ, D), 1.0)
    inp["mem"] = normal((BATCH, MEM_LEN, D), 1.0)
    inp["rel_bias"] = normal((REL_BUCKETS, DIL_HEADS), 0.5)
    inp["a_w_in"] = normal((nA, D, 2 * D), D ** -0.5)
    inp["a_b_in"] = normal((nA, 2 * D), 0.02)
    inp["a_dw"] = normal((nA, CONV_WIDTH, D), CONV_WIDTH ** -0.5)
    inp["a_dw_b"] = normal((nA, D), 0.02)
    inp["a_ln_g"] = 1.0 + normal((nA, D), 0.02)
    inp["a_ln_b"] = normal((nA, D), 0.02)
    inp["a_w_out"] = normal((nA, D, D), D ** -0.5 * DEEPNORM_BETA)
    inp["a_b_out"] = normal((nA, D), 0.02)
    inp["b_w_qkv"] = normal((nB, D, qkv_cols), D ** -0.5)
    inp["b_w_out"] = normal((nB, DIL_HEADS * DIL_HEAD_DIM, D),
                            (DIL_HEADS * DIL_HEAD_DIM) ** -0.5 * DEEPNORM_BETA)
    inp["c_mu"] = jax.random.uniform(next(ks), (nC, 6, D), F32)
    inp["c_w_rkv"] = normal((nC, 3, D, D), D ** -0.5)
    inp["c_w0"] = jnp.linspace(-6.5, -1.5, D, dtype=F32)[None] + normal((nC, D), 0.1)
    inp["c_w1"] = normal((nC, D, DECAY_LORA), D ** -0.5)
    inp["c_w2"] = normal((nC, DECAY_LORA, D), DECAY_LORA ** -0.5 * 0.1)
    inp["c_a0"] = normal((nC, D), 0.1)
    inp["c_a1"] = normal((nC, D, ICLR_LORA), D ** -0.5)
    inp["c_a2"] = normal((nC, ICLR_LORA, D), ICLR_LORA ** -0.5 * 0.1)
    inp["c_g1"] = normal((nC, D, GATE_LORA), D ** -0.5)
    inp["c_g2"] = normal((nC, GATE_LORA, D), GATE_LORA ** -0.5)
    inp["c_k_k"] = 0.85 + normal((nC, D), 0.02)
    inp["c_k_a"] = 1.0 + normal((nC, D), 0.02)
    inp["c_r_k"] = normal((nC, RWKV_HEADS, RWKV_HEAD), 0.1)
    inp["c_lnx_g"] = 1.0 + normal((nC, D), 0.02)
    inp["c_lnx_b"] = normal((nC, D), 0.02)
    inp["c_w_out"] = normal((nC, D, D), D ** -0.5 * DEEPNORM_BETA)
    inp["x_w_q"] = normal((DEPTH, D, D), D ** -0.5)
    inp["x_w_kv"] = normal((DEPTH, D, 2 * D), D ** -0.5)
    inp["x_w_out"] = normal((DEPTH, D, D), D ** -0.5 * DEEPNORM_BETA)
    inp["m_w1"] = normal((DEPTH, D, D_FF), D ** -0.5)
    inp["m_w2"] = normal((DEPTH, D_FF, D), D_FF ** -0.5 * DEEPNORM_BETA)
    inp["ln_g"] = 1.0 + normal((DEPTH, 3, D), 0.02)
    inp["ln_b"] = normal((DEPTH, 3, D), 0.02)
    return inp


def reference(x, mem, rel_bias,
              a_w_in, a_b_in, a_dw, a_dw_b, a_ln_g, a_ln_b, a_w_out, a_b_out,
              b_w_qkv, b_w_out,
              c_mu, c_w_rkv, c_w0, c_w1, c_w2, c_a0, c_a1, c_a2, c_g1, c_g2,
              c_k_k, c_k_a, c_r_k, c_lnx_g, c_lnx_b, c_w_out,
              x_w_q, x_w_kv, x_w_out, m_w1, m_w2, ln_g, ln_b):
    for i in range(DEPTH):
        kind, j = i % N_MIXERS, i // N_MIXERS
        if kind == 0:
            h = conv_module(x, a_w_in[j], a_b_in[j], a_dw[j], a_dw_b[j], a_ln_g[j], a_ln_b[j],
                            a_w_out[j], a_b_out[j])
        elif kind == 1:
            h = dilated_attention(x, b_w_qkv[j], b_w_out[j], rel_bias)
        else:
            h = rwkv7_time_mix(x, c_mu[j], c_w_rkv[j], c_w0[j], c_w1[j], c_w2[j], c_a0[j],
                               c_a1[j], c_a2[j], c_g1[j], c_g2[j], c_k_k[j], c_k_a[j], c_r_k[j],
                               c_lnx_g[j], c_lnx_b[j], c_w_out[j])
        x = layer_norm(DEEPNORM_ALPHA * x + h, ln_g[i, 0], ln_b[i, 0])
        h = memory_cross_attention(x, mem, x_w_q[i], x_w_kv[i], x_w_out[i])
        x = layer_norm(DEEPNORM_ALPHA * x + h, ln_g[i, 1], ln_b[i, 1])
        h = sq_relu_mlp(x, m_w1[i], m_w2[i])
        x = layer_norm(DEEPNORM_ALPHA * x + h, ln_g[i, 2], ln_b[i, 2])
    return x
```

```python
import functools
import math

import jax
import jax.numpy as jnp
from jax import lax
from jax.experimental import pallas as pl
from jax.experimental.pallas import tpu as pltpu

F32 = jnp.float32
BF16 = jnp.bfloat16
MXU_DTYPE = jnp.bfloat16

CONV_WIDTH = 31
DIL_GROUPS = ((128, 1), (512, 4), (2048, 16))
DIL_HEADS = 16
DIL_HEAD_DIM = 64
REL_BUCKETS = 32
REL_MAX_DIST = 2048
RWKV_HEAD = 64
RWKV_GN_EPS = 64e-5
XATTN_HEADS = 4
LN_EPS = 1e-5
NEG_BIG = -1e30

LANES = 128
SUBLANES = 8
VMEM_LIMIT_BYTES = 56 * 1024 * 1024

RWKV_CHUNK = 64
TRI_BASE = 8


def _cparams(*sem):
    return pltpu.CompilerParams(dimension_semantics=sem, vmem_limit_bytes=VMEM_LIMIT_BYTES)


def _mm(a, b):
    return jnp.dot(a.astype(MXU_DTYPE), b.astype(MXU_DTYPE), preferred_element_type=F32)


def _mm_nt(a, b):
    return lax.dot_general(a.astype(MXU_DTYPE), b.astype(MXU_DTYPE),
                           (((1,), (1,)), ((), ())), preferred_element_type=F32)


def _mm_tn(a, b):
    return _mm(a.T, b)


def _split3(x):
    hi = x.astype(BF16)
    r1 = x - hi.astype(F32)
    mid = r1.astype(BF16)
    lo = (r1 - mid.astype(F32)).astype(BF16)
    return hi, mid, lo


def _mm_exact_rhs(x, e):
    hi, mid, lo = _split3(x)
    dot = functools.partial(jnp.dot, preferred_element_type=F32)
    return dot(hi, e) + dot(mid, e) + dot(lo, e)


def _mm_exact_lhs(e, x):
    hi, mid, lo = _split3(x)
    dot = functools.partial(jnp.dot, preferred_element_type=F32)
    return dot(e, hi) + dot(e, mid) + dot(e, lo)


def _sigmoid(z):
    return 1.0 / (1.0 + jnp.exp(-z))


def _softplus(z):
    return jnp.maximum(z, 0.0) + jnp.log(1.0 + jnp.exp(-jnp.abs(z)))


def _layer_norm(z, g, b, eps=LN_EPS):
    mu = jnp.mean(z, axis=-1, keepdims=True)
    zc = z - mu
    var = jnp.mean(zc * zc, axis=-1, keepdims=True)
    return zc * lax.rsqrt(var + eps) * g + b


def _postnorm(alpha, x, h, g, b):
    return _layer_norm(alpha * x + h, g, b)


def _full(shape):
    nd = len(shape)
    return pl.BlockSpec(shape, lambda *_: (0,) * nd)


def _resident(shape):
    nd = len(shape)
    return pl.BlockSpec(shape, lambda *_: (0,) * nd, pipeline_mode=pl.Buffered(1))


def _row(v):
    return v.reshape(1, -1)


def _linear_kernel(x_ref, w_ref, o_ref):
    o_ref[...] = _mm(x_ref[...], w_ref[...]).astype(o_ref.dtype)


def _linear(x2d, w, out_dtype, tm, tn):
    m, k = x2d.shape
    n = w.shape[1]
    tm, tn = min(tm, m), min(tn, n)
    return pl.pallas_call(
        _linear_kernel,
        grid=(m // tm, n // tn),
        in_specs=[pl.BlockSpec((tm, k), lambda i, j: (i, 0)),
                  pl.BlockSpec((k, tn), lambda i, j: (0, j))],
        out_specs=pl.BlockSpec((tm, tn), lambda i, j: (i, j)),
        out_shape=jax.ShapeDtypeStruct((m, n), out_dtype),
        compiler_params=_cparams("parallel", "parallel"),
        name="linear",
    )(x2d, w)


def _xattn_kernel(x_ref, kv_ref, wq_ref, wo_ref, g_ref, b_ref, o_ref, oh_ref, *, heads, alpha):
    xb = x_ref[0]
    d = xb.shape[-1]
    e = d // heads
    q = (_mm(xb, wq_ref[...]) * (e ** -0.5)).astype(MXU_DTYPE)
    for h in range(heads):
        kh = kv_ref[0, :, h * e:(h + 1) * e]
        vh = kv_ref[0, :, d + h * e:d + (h + 1) * e]
        s = _mm_nt(q[:, h * e:(h + 1) * e], kh)
        m = jnp.max(s, axis=-1, keepdims=True)
        p = jnp.exp(s - m)
        den = jnp.sum(p, axis=-1, keepdims=True)
        oh_ref[:, h * e:(h + 1) * e] = _mm(p, vh) / den
    hout = _mm(oh_ref[...], wo_ref[...])
    o_ref[0] = _postnorm(alpha, xb, hout, g_ref[...], b_ref[...])


def _cross_attention(x, kv, wq, wo, g, b, alpha, tm=512):
    bsz, s, d = x.shape
    mlen = kv.shape[1]
    tm = min(tm, s)
    return pl.pallas_call(
        functools.partial(_xattn_kernel, heads=XATTN_HEADS, alpha=alpha),
        grid=(bsz, s // tm),
        in_specs=[pl.BlockSpec((1, tm, d), lambda bi, i: (bi, i, 0)),
                  pl.BlockSpec((1, mlen, 2 * d), lambda bi, i: (bi, 0, 0)),
                  _resident((d, d)), _resident((d, d)), _full((1, d)), _full((1, d))],
        out_specs=pl.BlockSpec((1, tm, d), lambda bi, i: (bi, i, 0)),
        out_shape=jax.ShapeDtypeStruct((bsz, s, d), F32),
        scratch_shapes=[pltpu.VMEM((tm, d), F32)],
        compiler_params=_cparams("parallel", "parallel"),
        name="cross_attention",
    )(x, kv, wq, wo, _row(g), _row(b))


def _mlp_kernel(x_ref, w1_ref, w2_ref, g_ref, b_ref, o_ref, *, fchunk, alpha):
    xb = x_ref[...]
    xm = xb.astype(MXU_DTYPE)
    ff = w1_ref.shape[1]
    acc = jnp.zeros(xb.shape, F32)
    for c in range(ff // fchunk):
        hid = jnp.dot(xm, w1_ref[:, c * fchunk:(c + 1) * fchunk], preferred_element_type=F32)
        hid = jnp.maximum(hid, 0.0)
        acc = acc + _mm(hid * hid, w2_ref[c * fchunk:(c + 1) * fchunk, :])
    o_ref[...] = _postnorm(alpha, xb, acc, g_ref[...], b_ref[...])


def _mlp(x2d, w1, w2, g, b, alpha, tm=512, fchunk=1024):
    n, d = x2d.shape
    ff = w1.shape[1]
    tm = min(tm, n)
    return pl.pallas_call(
        functools.partial(_mlp_kernel, fchunk=min(fchunk, ff), alpha=alpha),
        grid=(n // tm,),
        in_specs=[pl.BlockSpec((tm, d), lambda i: (i, 0)),
                  _resident((d, ff)), _resident((ff, d)), _full((1, d)), _full((1, d))],
        out_specs=pl.BlockSpec((tm, d), lambda i: (i, 0)),
        out_shape=jax.ShapeDtypeStruct((n, d), F32),
        compiler_params=_cparams("parallel"),
        name="sq_relu_mlp",
    )(x2d, w1, w2, _row(g), _row(b))


def _glu_kernel(x_ref, w_ref, b_ref, o_ref):
    h = _mm(x_ref[...], w_ref[...]) + b_ref[...]
    d = o_ref.shape[-1]
    o_ref[...] = h[:, :d] * _sigmoid(h[:, d:])


def _glu_proj(x2d, w, b, tm=512):
    n, d = x2d.shape
    tm = min(tm, n)
    return pl.pallas_call(
        _glu_kernel,
        grid=(n // tm,),
        in_specs=[pl.BlockSpec((tm, d), lambda i: (i, 0)), _resident((d, 2 * d)), _full((1, 2 * d))],
        out_specs=pl.BlockSpec((tm, d), lambda i: (i, 0)),
        out_shape=jax.ShapeDtypeStruct((n, d), F32),
        compiler_params=_cparams("parallel"),
        name="conv_glu_proj",
    )(x2d, w, _row(b))


CONV_HALO = 32


def _conv_kernel(u_ref, halo_ref, x_ref, dw_ref, dwb_ref, lng_ref, lnb_ref, wo_ref, bo_ref,
                 g_ref, b_ref, o_ref, cat_ref, *, alpha):
    i = pl.program_id(1)
    tm = u_ref.shape[1]
    cat_ref[0:CONV_HALO, :] = jnp.where(i == 0, 0.0, halo_ref[0])
    cat_ref[CONV_HALO:CONV_HALO + tm, :] = u_ref[0]
    acc = jnp.broadcast_to(dwb_ref[...], (tm, u_ref.shape[2]))
    off = CONV_HALO - (CONV_WIDTH - 1)
    for j in range(CONV_WIDTH):
        acc = acc + dw_ref[j:j + 1, :] * cat_ref[off + j:off + j + tm, :]
    z = _layer_norm(acc, lng_ref[...], lnb_ref[...])
    z = z * _sigmoid(z)
    hout = _mm(z, wo_ref[...]) + bo_ref[...]
    o_ref[0] = _postnorm(alpha, x_ref[0], hout, g_ref[...], b_ref[...])


def _conv_mixer(x, w_in, b_in, dw, dw_b, ln_g, ln_b, w_out, b_out, g, b, alpha, tm=256):
    bsz, s, d = x.shape
    u = _glu_proj(x.reshape(bsz * s, d), w_in, b_in).reshape(bsz, s, d)
    tm = min(tm, s)
    hb = tm // CONV_HALO
    return pl.pallas_call(
        functools.partial(_conv_kernel, alpha=alpha),
        grid=(bsz, s // tm),
        in_specs=[pl.BlockSpec((1, tm, d), lambda bi, i: (bi, i, 0)),
                  pl.BlockSpec((1, CONV_HALO, d), lambda bi, i: (bi, jnp.maximum(i * hb - 1, 0), 0)),
                  pl.BlockSpec((1, tm, d), lambda bi, i: (bi, i, 0)),
                  _full((CONV_WIDTH, d)), _full((1, d)), _full((1, d)), _full((1, d)),
                  _resident((d, d)), _full((1, d)), _full((1, d)), _full((1, d))],
        out_specs=pl.BlockSpec((1, tm, d), lambda bi, i: (bi, i, 0)),
        out_shape=jax.ShapeDtypeStruct((bsz, s, d), F32),
        scratch_shapes=[pltpu.VMEM((CONV_HALO + tm, d), F32)],
        compiler_params=_cparams("parallel", "parallel"),
        name="conv_dw_ln_out",
    )(u, u, x, dw, _row(dw_b), _row(ln_g), _row(ln_b), w_out, _row(b_out), _row(g), _row(b))


def _t5_causal_bucket(dist):
    n = jnp.maximum(dist, 0)
    max_exact = REL_BUCKETS // 2
    nf = jnp.maximum(n, 1).astype(F32)
    large = max_exact + (jnp.log(nf / max_exact) / math.log(REL_MAX_DIST / max_exact)
                         * (REL_BUCKETS - max_exact)).astype(jnp.int32)
    large = jnp.minimum(large, REL_BUCKETS - 1)
    return jnp.where(n < max_exact, n, large)


def _qkv_kernel(x_ref, w_ref, o_ref, *, qcols, qscale):
    acc = _mm(x_ref[0], w_ref[...])
    o_ref[0, 0, :, :qcols] = (acc[:, :qcols] * qscale).astype(o_ref.dtype)
    o_ref[0, 0, :, qcols:] = acc[:, qcols:].astype(o_ref.dtype)


def _qkv_proj(x, w_g, dil, tm=512):
    bsz, s, d = x.shape
    sub = s // dil
    ncol = w_g.shape[1]
    tm = min(tm, sub)
    xr = x.reshape(bsz, sub, dil * d)
    return pl.pallas_call(
        functools.partial(_qkv_kernel, qcols=ncol // 3, qscale=DIL_HEAD_DIM ** -0.5),
        grid=(bsz, dil, sub // tm),
        in_specs=[pl.BlockSpec((1, tm, d), lambda bi, c, i: (bi, i, c)),
                  _resident((d, ncol))],
        out_specs=pl.BlockSpec((1, 1, tm, ncol), lambda bi, c, i: (bi, c, i, 0)),
        out_shape=jax.ShapeDtypeStruct((bsz, dil, sub, ncol), MXU_DTYPE),
        compiler_params=_cparams("parallel", "parallel", "parallel"),
        name="dil_qkv_proj",
    )(xr, w_g)


def _dil_attn_kernel(bucket_ref, relb_ref, q_ref, k_ref, v_ref, kp_ref, vp_ref, o_ref, l_ref,
                     kc_ref, vc_ref, bias_ref, *, heads, blk):
    first_step = (pl.program_id(0) == 0) & (pl.program_id(1) == 0) & (pl.program_id(2) == 0)

    @pl.when(first_step)
    def _():
        bkt = bucket_ref[...]
        for h in range(heads):
            acc = jnp.full(bkt.shape, NEG_BIG, F32)
            for n in range(REL_BUCKETS):
                acc = jnp.where(bkt == n, relb_ref[n, h], acc)
            bias_ref[h] = acc

    i = pl.program_id(2)
    tq = q_ref.shape[2]
    kc_ref[0:blk, :] = kp_ref[0, 0]
    kc_ref[blk:blk + tq, :] = k_ref[0, 0]
    vc_ref[0:blk, :] = vp_ref[0, 0]
    vc_ref[blk:blk + tq, :] = v_ref[0, 0]
    lane = lax.broadcasted_iota(jnp.int32, (blk, LANES), 1)
    low = lane < DIL_HEAD_DIM
    kcol = lax.broadcasted_iota(jnp.int32, (blk, 2 * blk), 1)

    def body(jb, carry):
        r0 = pl.multiple_of(jb * blk, blk)
        start_mask = jnp.where((kcol < blk) & (i == 0) & (jb == 0), NEG_BIG, 0.0)
        for hp in range(heads // 2):
            cols = slice(hp * LANES, (hp + 1) * LANES)
            q2 = q_ref[0, 0, pl.ds(r0, blk), cols]
            k2 = kc_ref[pl.ds(r0, 2 * blk), cols]
            v2 = vc_ref[pl.ds(r0, 2 * blk), cols]
            outs, lses = [], []
            for hh in range(2):
                keep = low if hh == 0 else jnp.logical_not(low)
                qm = jnp.where(keep, q2, jnp.zeros_like(q2))
                s = _mm_nt(qm, k2) + bias_ref[2 * hp + hh] + start_mask
                m = jnp.max(s, axis=-1, keepdims=True)
                p = jnp.exp(s - m)
                den = jnp.sum(p, axis=-1, keepdims=True)
                outs.append(_mm(p, v2) / den)
                lses.append(m + jnp.log(den))
            o_ref[0, pl.ds(r0, blk), cols] = jnp.where(low, outs[0], outs[1])
            l_ref[0, pl.ds(r0, blk), cols] = jnp.where(low, lses[0], lses[1])
        return carry

    lax.fori_loop(0, tq // blk, body, 0)


def _dil_attention_group(qkv, rel_bias, window, dil, tq=512):
    bsz, _, sub, ncol = qkv.shape
    he = ncol // 3
    blk = window // dil
    tq = min(tq, sub)
    nsub = tq // blk
    qi = jnp.arange(blk)[:, None]
    kj = jnp.arange(2 * blk)[None, :]
    rel = qi + blk - kj
    bucket = jnp.where((rel >= 0) & (rel <= blk), _t5_causal_bucket(rel * dil), -1).astype(jnp.int32)

    def prev_map(col):
        return lambda bi, c, i: (bi, c, jnp.maximum(i * nsub - 1, 0), col)

    o, lse = pl.pallas_call(
        functools.partial(_dil_attn_kernel, heads=DIL_HEADS, blk=blk),
        grid=(bsz, dil, sub // tq),
        in_specs=[_full((blk, 2 * blk)),
                  pl.BlockSpec(memory_space=pltpu.SMEM),
                  pl.BlockSpec((1, 1, tq, he), lambda bi, c, i: (bi, c, i, 0)),
                  pl.BlockSpec((1, 1, tq, he), lambda bi, c, i: (bi, c, i, 1)),
                  pl.BlockSpec((1, 1, tq, he), lambda bi, c, i: (bi, c, i, 2)),
                  pl.BlockSpec((1, 1, blk, he), prev_map(1)),
                  pl.BlockSpec((1, 1, blk, he), prev_map(2))],
        out_specs=[pl.BlockSpec((1, tq, he), lambda bi, c, i: (bi, i, c)),
                   pl.BlockSpec((1, tq, he), lambda bi, c, i: (bi, i, c))],
        out_shape=[jax.ShapeDtypeStruct((bsz, sub, dil * he), F32),
                   jax.ShapeDtypeStruct((bsz, sub, dil * he), F32)],
        scratch_shapes=[pltpu.VMEM((blk + tq, he), MXU_DTYPE),
                        pltpu.VMEM((blk + tq, he), MXU_DTYPE),
                        pltpu.VMEM((DIL_HEADS, blk, 2 * blk), F32)],
        compiler_params=_cparams("arbitrary", "arbitrary", "arbitrary"),
        name="dil_attention",
    )(bucket, rel_bias, qkv, qkv, qkv, qkv, qkv)
    s = sub * dil
    return o.reshape(bsz * s, he), lse.reshape(bsz * s, he)


def _dil_out_kernel(o1_ref, o2_ref, o3_ref, l1_ref, l2_ref, l3_ref, x_ref, wo_ref, g_ref, b_ref,
                    out_ref, *, alpha):
    l1, l2, l3 = l1_ref[...], l2_ref[...], l3_ref[...]
    m = jnp.maximum(jnp.maximum(l1, l2), l3)
    e1, e2, e3 = jnp.exp(l1 - m), jnp.exp(l2 - m), jnp.exp(l3 - m)
    o = (e1 * o1_ref[...] + e2 * o2_ref[...] + e3 * o3_ref[...]) / (e1 + e2 + e3)
    hout = _mm(o, wo_ref[...])
    out_ref[...] = _postnorm(alpha, x_ref[...], hout, g_ref[...], b_ref[...])


def _dil_mixer(x, w_qkv, w_out, rel_bias, g, b, alpha, tm=256):
    bsz, s, d = x.shape
    he = DIL_HEADS * DIL_HEAD_DIM
    outs, lses = [], []
    for gi, (window, dil) in enumerate(DIL_GROUPS):
        qkv = _qkv_proj(x, w_qkv[:, gi * 3 * he:(gi + 1) * 3 * he], dil)
        o, lse = _dil_attention_group(qkv, rel_bias, window, dil)
        outs.append(o)
        lses.append(lse)
    n = bsz * s
    tm = min(tm, n)
    row = pl.BlockSpec((tm, he), lambda i: (i, 0))
    out = pl.pallas_call(
        functools.partial(_dil_out_kernel, alpha=alpha),
        grid=(n // tm,),
        in_specs=[row] * 6 + [pl.BlockSpec((tm, d), lambda i: (i, 0)),
                              _resident((he, d)), _full((1, d)), _full((1, d))],
        out_specs=pl.BlockSpec((tm, d), lambda i: (i, 0)),
        out_shape=jax.ShapeDtypeStruct((n, d), F32),
        compiler_params=_cparams("parallel"),
        name="dil_combine_out",
    )(*outs, *lses, x.reshape(n, d), w_out, _row(g), _row(b))
    return out.reshape(bsz, s, d)


def _head_sum(z, e1_ref, e2_ref):
    return _mm_exact_rhs(_mm_exact_rhs(z, e1_ref[...]), e2_ref[...])


def _rwkv_prep_kernel(x_ref, xh_ref, mu_ref, wrkv_ref, w0_ref, w1_ref, w2_ref, a0_ref, a1_ref,
                      a2_ref, g1_ref, g2_ref, kk_ref, ka_ref, e1_ref, e2_ref,
                      r_o, k_o, v_o, lw_o, kn_o, b_o, g_o):
    i = pl.program_id(1)
    x = x_ref[0]
    tm = x.shape[0]
    prev_row = jnp.where(i == 0, 0.0, xh_ref[0, SUBLANES - 1:SUBLANES, :])
    rows = lax.broadcasted_iota(jnp.int32, (tm, 1), 0)
    xprev = jnp.where(rows == 0, prev_row, pltpu.roll(x, 1, 0))
    xx = xprev - x

    def mix(j):
        return x + xx * mu_ref[j:j + 1, :]

    r = _mm(mix(0), wrkv_ref[0])
    k = _mm(mix(1), wrkv_ref[1])
    v = _mm(mix(2), wrkv_ref[2])
    wl = w0_ref[...] + _mm(jnp.tanh(_mm(mix(3), w1_ref[...])), w2_ref[...])
    w_log = -_softplus(-wl) - 0.5
    a = _sigmoid(a0_ref[...] + _mm(_mm(mix(4), a1_ref[...]), a2_ref[...]))
    g = _mm(_sigmoid(_mm(mix(5), g1_ref[...])), g2_ref[...])
    kk = k * kk_ref[...]
    norm = jnp.sqrt(_head_sum(kk * kk, e1_ref, e2_ref))
    kk = kk / jnp.maximum(norm, 1e-12)
    r_o[0] = r
    k_o[0] = k * (1.0 + (a - 1.0) * ka_ref[...])
    v_o[0] = v
    lw_o[0] = -jnp.exp(w_log)
    kn_o[0] = kk
    b_o[0] = kk * a
    g_o[0] = g


def _tri_inv(a, blk):
    n = a.shape[0]
    r = lax.broadcasted_iota(jnp.int32, (n, n), 0)
    c = lax.broadcasted_iota(jnp.int32, (n, n), 1)

    def same(bs):
        sh = int(math.log2(bs))
        return lax.shift_right_logical(r, sh) == lax.shift_right_logical(c, sh)

    ad = jnp.where(same(TRI_BASE), a, 0.0)
    t = jnp.where(r == c, 1.0, 0.0) + ad
    p = ad
    m = 1
    while 2 * m < TRI_BASE:
        p = _mm(p, p)
        t = t + _mm(p, t)
        m *= 2
    bs = TRI_BASE
    while bs < blk:
        off = jnp.where(same(2 * bs) & jnp.logical_not(same(bs)), a, 0.0)
        t = t + _mm(_mm(t, off), t)
        bs *= 2
    return t


def _rwkv_scan_kernel(r_ref, k_ref, v_ref, lw_ref, kn_ref, b_ref, y_ref, st_ref, *, chunk):
    @pl.when(pl.program_id(2) == 0)
    def _():
        st_ref[...] = jnp.zeros_like(st_ref)

    L = chunk
    n2 = 2 * L
    lane = lax.broadcasted_iota(jnp.int32, (L, LANES), 1)
    head0 = lane < RWKV_HEAD
    rr = lax.broadcasted_iota(jnp.int32, (n2, n2), 0)
    cc = lax.broadcasted_iota(jnp.int32, (n2, n2), 1)
    rt, ct = rr & (L - 1), cc & (L - 1)
    strict, incl, eye = rt > ct, rt >= ct, rr == cc
    tri = (lax.broadcasted_iota(jnp.int32, (L, L), 0)
           >= lax.broadcasted_iota(jnp.int32, (L, L), 1)).astype(BF16)

    def stack(z):
        return jnp.concatenate([jnp.where(head0, z, 0.0), jnp.where(head0, 0.0, z)], axis=0)

    for ch in range(r_ref.shape[1] // L):
        sl = slice(ch * L, (ch + 1) * L)
        r, k, v = r_ref[0, sl, :], k_ref[0, sl, :], v_ref[0, sl, :]
        lw, kn, b = lw_ref[0, sl, :], kn_ref[0, sl, :], b_ref[0, sl, :]
        c = _mm_exact_lhs(tri, lw)
        c_last = c[L - 1:L, :]
        e_inv = jnp.exp(-c)
        e_end = jnp.exp(c_last - c)
        a_s = stack(-kn * jnp.exp(c - lw))
        r_s = stack(r * jnp.exp(c))
        bh_s = stack(b * e_inv)
        kh_s = stack(k * e_inv)
        bb_s = stack(b * e_end)
        kb_s = stack(k * e_end)
        v_s = stack(v)
        m = _mm_nt(jnp.concatenate([a_s, r_s], axis=0), jnp.concatenate([bh_s, kh_s], axis=0))
        a_ab = jnp.where(strict, m[:n2, :n2], 0.0)
        a_ak = jnp.where(strict, m[:n2, n2:], 0.0)
        b_rb = jnp.where(incl, m[n2:, :n2], 0.0)
        b_rk = jnp.where(incl, m[n2:, n2:], 0.0)
        t = _tri_inv(a_ab, L)
        avbv = _mm(jnp.concatenate([a_ak, b_rk], axis=0), v_s)
        tu = _mm(t, jnp.concatenate([a_s, avbv[:n2]], axis=1))
        bu = _mm(b_rb, tu)
        rmat = r_s + bu[:, :LANES]
        y0 = bu[:, LANES:] + avbv[n2:]
        gh = _mm_tn(bb_s, tu)
        gmat = gh[:, :LANES] + jnp.where(eye, jnp.exp(c_last), 0.0)
        h0 = gh[:, LANES:] + _mm_tn(kb_s, v_s)
        rg = _mm(jnp.concatenate([rmat, gmat], axis=0), st_ref[...])
        y2 = rg[:n2] + y0
        st_ref[...] = rg[n2:] + h0
        y_ref[0, sl, :] = y2[:L] + y2[L:]


def _rwkv_post_kernel(y_ref, r_ref, k_ref, v_ref, gate_ref, x_ref, lg_ref, lb_ref, rk_ref,
                      e1_ref, e2_ref, wo_ref, g_ref, b_ref, o_ref, *, alpha):
    y = y_ref[...]
    inv_n = 1.0 / RWKV_HEAD
    mu = _head_sum(y, e1_ref, e2_ref) * inv_n
    yc = y - mu
    var = _head_sum(yc * yc, e1_ref, e2_ref) * inv_n
    yn = yc * lax.rsqrt(var + RWKV_GN_EPS) * lg_ref[...] + lb_ref[...]
    bonus = _head_sum(r_ref[...] * k_ref[...] * rk_ref[...], e1_ref, e2_ref) * v_ref[...]
    hout = _mm((yn + bonus) * gate_ref[...], wo_ref[...])
    o_ref[...] = _postnorm(alpha, x_ref[...], hout, g_ref[...], b_ref[...])


def _rwkv_mixer(x, mu, w_rkv, w0, w1, w2, a0, a1, a2, g1, g2, k_k, k_a, r_k, lnx_g, lnx_b,
                w_out, g, b, alpha, tm=256, tblk=256):
    bsz, s, d = x.shape
    nh = d // RWKV_HEAD
    tm = min(tm, s)
    chan_head = jnp.arange(d)[:, None] // RWKV_HEAD == jnp.arange(LANES)[None, :]
    e1 = chan_head.astype(BF16)
    e2 = chan_head.T.astype(BF16)
    tile = pl.BlockSpec((1, tm, d), lambda bi, i: (bi, i, 0))
    hb = tm // SUBLANES
    lora = w1.shape[1]
    glora = g1.shape[1]
    outs = pl.pallas_call(
        _rwkv_prep_kernel,
        grid=(bsz, s // tm),
        in_specs=[tile,
                  pl.BlockSpec((1, SUBLANES, d), lambda bi, i: (bi, jnp.maximum(i * hb - 1, 0), 0)),
                  _full((6, d)), _resident((3, d, d)), _full((1, d)),
                  _full((d, lora)), _full((lora, d)), _full((1, d)),
                  _full((d, a1.shape[1])), _full((a2.shape[0], d)),
                  _full((d, glora)), _full((glora, d)),
                  _full((1, d)), _full((1, d)), _full((d, LANES)), _full((LANES, d))],
        out_specs=[tile] * 7,
        out_shape=[jax.ShapeDtypeStruct((bsz, s, d), F32)] * 7,
        compiler_params=_cparams("parallel", "parallel"),
        name="rwkv_prep",
    )(x, x, mu, w_rkv, _row(w0), w1, w2, _row(a0), a1, a2, g1, g2, _row(k_k), _row(k_a), e1, e2)
    r, k, v, lw, kn, bvec, gate = outs

    tblk = min(tblk, s)
    blk = pl.BlockSpec((1, tblk, LANES), lambda bi, hp, i: (bi, i, hp))
    y = pl.pallas_call(
        functools.partial(_rwkv_scan_kernel, chunk=RWKV_CHUNK),
        grid=(bsz, nh // 2, s // tblk),
        in_specs=[blk] * 6,
        out_specs=blk,
        out_shape=jax.ShapeDtypeStruct((bsz, s, d), F32),
        scratch_shapes=[pltpu.VMEM((LANES, LANES), F32)],
        compiler_params=_cparams("parallel", "parallel", "arbitrary"),
        name="rwkv_scan",
    )(r, k, v, lw, kn, bvec)

    n = bsz * s
    tm2 = min(tm, n)
    row = pl.BlockSpec((tm2, d), lambda i: (i, 0))
    flat = lambda z: z.reshape(n, d)
    out = pl.pallas_call(
        functools.partial(_rwkv_post_kernel, alpha=alpha),
        grid=(n // tm2,),
        in_specs=[row] * 6 + [_full((1, d))] * 3 + [_full((d, LANES)), _full((LANES, d)),
                                                    _resident((d, d)), _full((1, d)), _full((1, d))],
        out_specs=row,
        out_shape=jax.ShapeDtypeStruct((n, d), F32),
        compiler_params=_cparams("parallel"),
        name="rwkv_post",
    )(flat(y), flat(r), flat(k), flat(v), flat(gate), flat(x), _row(lnx_g), _row(lnx_b),
      _row(r_k), e1, e2, w_out, _row(g), _row(b))
    return out.reshape(bsz, s, d)


def kernel(x, mem, rel_bias, a_w_in, a_b_in, a_dw, a_dw_b, a_ln_g, a_ln_b, a_w_out, a_b_out, b_w_qkv, b_w_out, c_mu, c_w_rkv, c_w0, c_w1, c_w2, c_a0, c_a1, c_a2, c_g1, c_g2, c_k_k, c_k_a, c_r_k, c_lnx_g, c_lnx_b, c_w_out, x_w_q, x_w_kv, x_w_out, m_w1, m_w2, ln_g, ln_b):
    depth = ln_g.shape[0]
    alpha = (2 * depth) ** 0.25
    bsz, s, d = x.shape
    mlen = mem.shape[1]
    w = lambda t: t.astype(MXU_DTYPE)
    mem2d = mem.reshape(bsz * mlen, d)
    for i in range(depth):
        kind, j = i % 3, i // 3
        if kind == 0:
            x = _conv_mixer(x, w(a_w_in[j]), a_b_in[j], a_dw[j], a_dw_b[j], a_ln_g[j], a_ln_b[j],
                            w(a_w_out[j]), a_b_out[j], ln_g[i, 0], ln_b[i, 0], alpha)
        elif kind == 1:
            x = _dil_mixer(x, w(b_w_qkv[j]), w(b_w_out[j]), rel_bias, ln_g[i, 0], ln_b[i, 0], alpha)
        else:
            x = _rwkv_mixer(x, c_mu[j], w(c_w_rkv[j]), c_w0[j], w(c_w1[j]), w(c_w2[j]), c_a0[j],
                            w(c_a1[j]), w(c_a2[j]), w(c_g1[j]), w(c_g2[j]), c_k_k[j], c_k_a[j],
                            c_r_k[j], c_lnx_g[j], c_lnx_b[j], w(c_w_out[j]),
                            ln_g[i, 0], ln_b[i, 0], alpha)
        kv = _linear(mem2d, w(x_w_kv[i]), MXU_DTYPE, 512, 1024).reshape(bsz, mlen, 2 * d)
        x = _cross_attention(x, kv, w(x_w_q[i]), w(x_w_out[i]), ln_g[i, 1], ln_b[i, 1], alpha)
        x = _mlp(x.reshape(bsz * s, d), w(m_w1[i]), w(m_w2[i]), ln_g[i, 2], ln_b[i, 2],
                 alpha).reshape(bsz, s, d)
    return x
```

```python
import functools
import math

import jax
import jax.numpy as jnp
from jax import lax
from jax.experimental import pallas as pl
from jax.experimental.pallas import tpu as pltpu

F32 = jnp.float32
BF16 = jnp.bfloat16
MXU_DTYPE = jnp.bfloat16

CONV_WIDTH = 31
DIL_GROUPS = ((128, 1), (512, 4), (2048, 16))
DIL_HEADS = 16
DIL_HEAD_DIM = 64
REL_BUCKETS = 32
REL_MAX_DIST = 2048
RWKV_HEAD = 64
RWKV_GN_EPS = 64e-5
XATTN_HEADS = 4
LN_EPS = 1e-5
NEG_BIG = -1e30

LANES = 128
SUBLANES = 8
VMEM_LIMIT_BYTES = 56 * 1024 * 1024

RWKV_CHUNK = 64
TRI_BASE = 8


def _cparams(*sem):
    return pltpu.CompilerParams(dimension_semantics=sem, vmem_limit_bytes=VMEM_LIMIT_BYTES)


def _mm(a, b):
    return jnp.dot(a.astype(MXU_DTYPE), b.astype(MXU_DTYPE), preferred_element_type=F32)


def _mm_nt(a, b):
    return lax.dot_general(a.astype(MXU_DTYPE), b.astype(MXU_DTYPE),
                           (((1,), (1,)), ((), ())), preferred_element_type=F32)


def _mm_tn(a, b):
    return _mm(a.T, b)


def _split3(x):
    hi = x.astype(BF16)
    r1 = x - hi.astype(F32)
    mid = r1.astype(BF16)
    lo = (r1 - mid.astype(F32)).astype(BF16)
    return hi, mid, lo


def _mm_exact_rhs(x, e):
    hi, mid, lo = _split3(x)
    dot = functools.partial(jnp.dot, preferred_element_type=F32)
    return dot(hi, e) + dot(mid, e) + dot(lo, e)


def _mm_exact_lhs(e, x):
    hi, mid, lo = _split3(x)
    dot = functools.partial(jnp.dot, preferred_element_type=F32)
    return dot(e, hi) + dot(e, mid) + dot(e, lo)


def _sigmoid(z):
    return 1.0 / (1.0 + jnp.exp(-z))


def _softplus(z):
    return jnp.maximum(z, 0.0) + jnp.log(1.0 + jnp.exp(-jnp.abs(z)))


def _layer_norm(z, g, b, eps=LN_EPS):
    mu = jnp.mean(z, axis=-1, keepdims=True)
    zc = z - mu
    var = jnp.mean(zc * zc, axis=-1, keepdims=True)
    return zc * lax.rsqrt(var + eps) * g + b


def _postnorm(alpha, x, h, g, b):
    return _layer_norm(alpha * x + h, g, b)


def _full(shape):
    nd = len(shape)
    return pl.BlockSpec(shape, lambda *_: (0,) * nd)


def _resident(shape):
    nd = len(shape)
    return pl.BlockSpec(shape, lambda *_: (0,) * nd, pipeline_mode=pl.Buffered(1))


def _row(v):
    return v.reshape(1, -1)


def _linear_kernel(x_ref, w_ref, o_ref):
    o_ref[...] = _mm(x_ref[...], w_ref[...]).astype(o_ref.dtype)


def _linear(x2d, w, out_dtype, tm, tn):
    m, k = x2d.shape
    n = w.shape[1]
    tm, tn = min(tm, m), min(tn, n)
    return pl.pallas_call(
        _linear_kernel,
        grid=(m // tm, n // tn),
        in_specs=[pl.BlockSpec((tm, k), lambda i, j: (i, 0)),
                  pl.BlockSpec((k, tn), lambda i, j: (0, j))],
        out_specs=pl.BlockSpec((tm, tn), lambda i, j: (i, j)),
        out_shape=jax.ShapeDtypeStruct((m, n), out_dtype),
        compiler_params=_cparams("parallel", "parallel"),
        name="linear",
    )(x2d, w)


def _xattn_kernel(x_ref, kv_ref, wq_ref, wo_ref, g_ref, b_ref, o_ref, oh_ref, *, heads, alpha):
    xb = x_ref[0]
    d = xb.shape[-1]
    e = d // heads
    q = (_mm(xb, wq_ref[...]) * (e ** -0.5)).astype(MXU_DTYPE)
    for h in range(heads):
        kh = kv_ref[0, :, h * e:(h + 1) * e]
        vh = kv_ref[0, :, d + h * e:d + (h + 1) * e]
        s = _mm_nt(q[:, h * e:(h + 1) * e], kh)
        m = jnp.max(s, axis=-1, keepdims=True)
        p = jnp.exp(s - m)
        den = jnp.sum(p, axis=-1, keepdims=True)
        oh_ref[:, h * e:(h + 1) * e] = _mm(p, vh) / den
    hout = _mm(oh_ref[...], wo_ref[...])
    o_ref[0] = _postnorm(alpha, xb, hout, g_ref[...], b_ref[...])


def _cross_attention(x, kv, wq, wo, g, b, alpha, tm=512):
    bsz, s, d = x.shape
    mlen = kv.shape[1]
    tm = min(tm, s)
    return pl.pallas_call(
        functools.partial(_xattn_kernel, heads=XATTN_HEADS, alpha=alpha),
        grid=(bsz, s // tm),
        in_specs=[pl.BlockSpec((1, tm, d), lambda bi, i: (bi, i, 0)),
                  pl.BlockSpec((1, mlen, 2 * d), lambda bi, i: (bi, 0, 0)),
                  _resident((d, d)), _resident((d, d)), _full((1, d)), _full((1, d))],
        out_specs=pl.BlockSpec((1, tm, d), lambda bi, i: (bi, i, 0)),
        out_shape=jax.ShapeDtypeStruct((bsz, s, d), F32),
        scratch_shapes=[pltpu.VMEM((tm, d), F32)],
        compiler_params=_cparams("parallel", "parallel"),
        name="cross_attention",
    )(x, kv, wq, wo, _row(g), _row(b))


def _mlp_kernel(x_ref, w1_ref, w2_ref, g_ref, b_ref, o_ref, *, fchunk, alpha):
    xb = x_ref[...]
    xm = xb.astype(MXU_DTYPE)
    ff = w1_ref.shape[1]
    acc = jnp.zeros(xb.shape, F32)
    for c in range(ff // fchunk):
        hid = jnp.dot(xm, w1_ref[:, c * fchunk:(c + 1) * fchunk], preferred_element_type=F32)
        hid = jnp.maximum(hid, 0.0)
        acc = acc + _mm(hid * hid, w2_ref[c * fchunk:(c + 1) * fchunk, :])
    o_ref[...] = _postnorm(alpha, xb, acc, g_ref[...], b_ref[...])


def _mlp(x2d, w1, w2, g, b, alpha, tm=512, fchunk=1024):
    n, d = x2d.shape
    ff = w1.shape[1]
    tm = min(tm, n)
    return pl.pallas_call(
        functools.partial(_mlp_kernel, fchunk=min(fchunk, ff), alpha=alpha),
        grid=(n // tm,),
        in_specs=[pl.BlockSpec((tm, d), lambda i: (i, 0)),
                  _resident((d, ff)), _resident((ff, d)), _full((1, d)), _full((1, d))],
        out_specs=pl.BlockSpec((tm, d), lambda i: (i, 0)),
        out_shape=jax.ShapeDtypeStruct((n, d), F32),
        compiler_params=_cparams("parallel"),
        name="sq_relu_mlp",
    )(x2d, w1, w2, _row(g), _row(b))


def _glu_kernel(x_ref, w_ref, b_ref, o_ref):
    h = _mm(x_ref[...], w_ref[...]) + b_ref[...]
    d = o_ref.shape[-1]
    o_ref[...] = h[:, :d] * _sigmoid(h[:, d:])


def _glu_proj(x2d, w, b, tm=512):
    n, d = x2d.shape
    tm = min(tm, n)
    return pl.pallas_call(
        _glu_kernel,
        grid=(n // tm,),
        in_specs=[pl.BlockSpec((tm, d), lambda i: (i, 0)), _resident((d, 2 * d)), _full((1, 2 * d))],
        out_specs=pl.BlockSpec((tm, d), lambda i: (i, 0)),
        out_shape=jax.ShapeDtypeStruct((n, d), F32),
        compiler_params=_cparams("parallel"),
        name="conv_glu_proj",
    )(x2d, w, _row(b))


CONV_HALO = 32


def _conv_kernel(u_ref, halo_ref, x_ref, dw_ref, dwb_ref, lng_ref, lnb_ref, wo_ref, bo_ref,
                 g_ref, b_ref, o_ref, cat_ref, sh_ref, *, alpha):
    i = pl.program_id(1)
    tm = u_ref.shape[1]
    cat_ref[0:CONV_HALO, :] = jnp.where(i == 0, 0.0, halo_ref[0])
    cat_ref[CONV_HALO:CONV_HALO + tm, :] = u_ref[0]
    acc = jnp.broadcast_to(dwb_ref[...], (tm, u_ref.shape[2]))
    off = CONV_HALO - (CONV_WIDTH - 1)
    for phase in range(SUBLANES):
        taps = [j for j in range(CONV_WIDTH) if (off + j) % SUBLANES == phase]
        span = max(off + j for j in taps) - phase + tm
        src = cat_ref
        if phase:
            sh_ref[0:span, :] = cat_ref[phase:phase + span, :]
            src = sh_ref
        for j in taps:
            a = off + j - phase
            acc = acc + dw_ref[j:j + 1, :] * src[a:a + tm, :]
    z = _layer_norm(acc, lng_ref[...], lnb_ref[...])
    z = z * _sigmoid(z)
    hout = _mm(z, wo_ref[...]) + bo_ref[...]
    o_ref[0] = _postnorm(alpha, x_ref[0], hout, g_ref[...], b_ref[...])


def _conv_mixer(x, w_in, b_in, dw, dw_b, ln_g, ln_b, w_out, b_out, g, b, alpha, tm=256):
    bsz, s, d = x.shape
    u = _glu_proj(x.reshape(bsz * s, d), w_in, b_in).reshape(bsz, s, d)
    tm = min(tm, s)
    hb = tm // CONV_HALO
    return pl.pallas_call(
        functools.partial(_conv_kernel, alpha=alpha),
        grid=(bsz, s // tm),
        in_specs=[pl.BlockSpec((1, tm, d), lambda bi, i: (bi, i, 0)),
                  pl.BlockSpec((1, CONV_HALO, d), lambda bi, i: (bi, jnp.maximum(i * hb - 1, 0), 0)),
                  pl.BlockSpec((1, tm, d), lambda bi, i: (bi, i, 0)),
                  _full((CONV_WIDTH, d)), _full((1, d)), _full((1, d)), _full((1, d)),
                  _resident((d, d)), _full((1, d)), _full((1, d)), _full((1, d))],
        out_specs=pl.BlockSpec((1, tm, d), lambda bi, i: (bi, i, 0)),
        out_shape=jax.ShapeDtypeStruct((bsz, s, d), F32),
        scratch_shapes=[pltpu.VMEM((CONV_HALO + tm, d), F32), pltpu.VMEM((CONV_HALO + tm, d), F32)],
        compiler_params=_cparams("parallel", "parallel"),
        name="conv_dw_ln_out",
    )(u, u, x, dw, _row(dw_b), _row(ln_g), _row(ln_b), w_out, _row(b_out), _row(g), _row(b))


def _t5_causal_bucket(dist):
    n = jnp.maximum(dist, 0)
    max_exact = REL_BUCKETS // 2
    nf = jnp.maximum(n, 1).astype(F32)
    large = max_exact + (jnp.log(nf / max_exact) / math.log(REL_MAX_DIST / max_exact)
                         * (REL_BUCKETS - max_exact)).astype(jnp.int32)
    large = jnp.minimum(large, REL_BUCKETS - 1)
    return jnp.where(n < max_exact, n, large)


def _qkv_kernel(x_ref, w_ref, o_ref, acc_ref, *, dil, qcols, qscale):
    acc = _mm(x_ref[0], w_ref[...])
    if dil == 1:
        o_ref[0, 0, :, :qcols] = (acc[:, :qcols] * qscale).astype(o_ref.dtype)
        o_ref[0, 0, :, qcols:] = acc[:, qcols:].astype(o_ref.dtype)
        return
    n = acc_ref.shape[1] // dil
    for j in range(acc_ref.shape[0]):
        cols = slice(j * LANES, (j + 1) * LANES)
        acc_ref[j] = acc[:, cols] * (qscale if j * LANES < qcols else 1.0)
        for c in range(dil):
            o_ref[0, c, :, cols] = acc_ref[j, pl.ds(c, n, stride=dil), :].astype(o_ref.dtype)


def _qkv_proj(x, w_g, dil, tm=512):
    bsz, s, d = x.shape
    sub = s // dil
    ncol = w_g.shape[1]
    tm = min(tm, s)
    return pl.pallas_call(
        functools.partial(_qkv_kernel, dil=dil, qcols=ncol // 3, qscale=DIL_HEAD_DIM ** -0.5),
        grid=(bsz, s // tm),
        in_specs=[pl.BlockSpec((1, tm, d), lambda bi, i: (bi, i, 0)),
                  _resident((d, ncol))],
        out_specs=pl.BlockSpec((1, dil, tm // dil, ncol), lambda bi, i: (bi, 0, i, 0)),
        out_shape=jax.ShapeDtypeStruct((bsz, dil, sub, ncol), MXU_DTYPE),
        scratch_shapes=[pltpu.VMEM((ncol // LANES, tm, LANES), F32)],
        compiler_params=_cparams("parallel", "parallel"),
        name="dil_qkv_proj",
    )(x, w_g)


def _dil_attn_kernel(bucket_ref, relb_ref, q_ref, k_ref, v_ref, kp_ref, vp_ref, o_ref, l_ref,
                     kc_ref, vc_ref, bias_ref, *, heads, blk):
    first_step = (pl.program_id(0) == 0) & (pl.program_id(1) == 0) & (pl.program_id(2) == 0)

    @pl.when(first_step)
    def _():
        bkt = bucket_ref[...]
        kcol = lax.broadcasted_iota(jnp.int32, bkt.shape, 1)
        for h in range(heads):
            acc = jnp.full(bkt.shape, NEG_BIG, F32)
            for n in range(REL_BUCKETS):
                acc = jnp.where(bkt == n, relb_ref[n, h], acc)
            bias_ref[h] = acc
            bias_ref[heads + h] = jnp.where(kcol < blk, NEG_BIG, acc)

    i = pl.program_id(2)
    tq = q_ref.shape[2]
    kc_ref[0:blk, :] = kp_ref[0, 0]
    kc_ref[blk:blk + tq, :] = k_ref[0, 0]
    vc_ref[0:blk, :] = vp_ref[0, 0]
    vc_ref[blk:blk + tq, :] = v_ref[0, 0]
    lane = lax.broadcasted_iota(jnp.int32, (blk, LANES), 1)
    low = lane < DIL_HEAD_DIM

    def body(jb, carry):
        r0 = pl.multiple_of(jb * blk, blk)
        table = jnp.where((i == 0) & (jb == 0), heads, 0)
        lse_all = jnp.zeros((blk, LANES), F32)
        for hp in range(heads // 2):
            cols = slice(hp * LANES, (hp + 1) * LANES)
            q2 = q_ref[0, 0, pl.ds(r0, blk), cols]
            k2 = kc_ref[pl.ds(r0, 2 * blk), cols]
            v2 = vc_ref[pl.ds(r0, 2 * blk), cols]
            outs = []
            for hh in range(2):
                h = 2 * hp + hh
                keep = low if hh == 0 else jnp.logical_not(low)
                qm = jnp.where(keep, q2, jnp.zeros_like(q2))
                s = _mm_nt(qm, k2) + bias_ref[table + h]
                m = jnp.max(s, axis=-1, keepdims=True)
                p = jnp.exp(s - m)
                den = jnp.sum(p, axis=-1, keepdims=True)
                outs.append(_mm(p, v2) / den)
                lse_all = jnp.where(lane == h, m + jnp.log(den), lse_all)
            o_ref[0, pl.ds(r0, blk), cols] = jnp.where(low, outs[0], outs[1]).astype(o_ref.dtype)
        l_ref[0, pl.ds(r0, blk), :] = lse_all
        return carry

    lax.fori_loop(0, tq // blk, body, 0)


def _dil_attention_group(qkv, rel_bias, window, dil, tq=512):
    bsz, _, sub, ncol = qkv.shape
    he = ncol // 3
    blk = window // dil
    tq = min(tq, sub)
    nsub = tq // blk
    qi = jnp.arange(blk)[:, None]
    kj = jnp.arange(2 * blk)[None, :]
    rel = qi + blk - kj
    bucket = jnp.where((rel >= 0) & (rel <= blk), _t5_causal_bucket(rel * dil), -1).astype(jnp.int32)

    def prev_map(col):
        return lambda bi, c, i: (bi, c, jnp.maximum(i * nsub - 1, 0), col)

    o, lse = pl.pallas_call(
        functools.partial(_dil_attn_kernel, heads=DIL_HEADS, blk=blk),
        grid=(bsz, dil, sub // tq),
        in_specs=[_full((blk, 2 * blk)),
                  pl.BlockSpec(memory_space=pltpu.SMEM),
                  pl.BlockSpec((1, 1, tq, he), lambda bi, c, i: (bi, c, i, 0)),
                  pl.BlockSpec((1, 1, tq, he), lambda bi, c, i: (bi, c, i, 1)),
                  pl.BlockSpec((1, 1, tq, he), lambda bi, c, i: (bi, c, i, 2)),
                  pl.BlockSpec((1, 1, blk, he), prev_map(1)),
                  pl.BlockSpec((1, 1, blk, he), prev_map(2))],
        out_specs=[pl.BlockSpec((1, tq, he), lambda bi, c, i: (bi, i, c)),
                   pl.BlockSpec((1, tq, LANES), lambda bi, c, i: (bi, i, c))],
        out_shape=[jax.ShapeDtypeStruct((bsz, sub, dil * he), MXU_DTYPE),
                   jax.ShapeDtypeStruct((bsz, sub, dil * LANES), F32)],
        scratch_shapes=[pltpu.VMEM((blk + tq, he), MXU_DTYPE),
                        pltpu.VMEM((blk + tq, he), MXU_DTYPE),
                        pltpu.VMEM((2 * DIL_HEADS, blk, 2 * blk), F32)],
        compiler_params=_cparams("arbitrary", "arbitrary", "arbitrary"),
        name="dil_attention",
    )(bucket, rel_bias, qkv, qkv, qkv, qkv, qkv)
    s = sub * dil
    return o.reshape(bsz * s, he), lse.reshape(bsz * s, LANES)


def _dil_out_kernel(o1_ref, o2_ref, o3_ref, l1_ref, l2_ref, l3_ref, x_ref, e_ref, wo_ref, g_ref,
                    b_ref, out_ref, *, alpha):
    l1, l2, l3 = l1_ref[...], l2_ref[...], l3_ref[...]
    m = jnp.maximum(jnp.maximum(l1, l2), l3)
    e1, e2, e3 = jnp.exp(l1 - m), jnp.exp(l2 - m), jnp.exp(l3 - m)
    inv = 1.0 / (e1 + e2 + e3)
    o = (_mm_exact_rhs(e1 * inv, e_ref[...]) * o1_ref[...]
         + _mm_exact_rhs(e2 * inv, e_ref[...]) * o2_ref[...]
         + _mm_exact_rhs(e3 * inv, e_ref[...]) * o3_ref[...])
    hout = _mm(o, wo_ref[...])
    out_ref[...] = _postnorm(alpha, x_ref[...], hout, g_ref[...], b_ref[...])


def _dil_mixer(x, w_qkv, w_out, rel_bias, g, b, alpha, tm=256):
    bsz, s, d = x.shape
    he = DIL_HEADS * DIL_HEAD_DIM
    outs, lses = [], []
    for gi, (window, dil) in enumerate(DIL_GROUPS):
        qkv = _qkv_proj(x, w_qkv[:, gi * 3 * he:(gi + 1) * 3 * he], dil)
        o, lse = _dil_attention_group(qkv, rel_bias, window, dil)
        outs.append(o)
        lses.append(lse)
    n = bsz * s
    tm = min(tm, n)
    row = pl.BlockSpec((tm, he), lambda i: (i, 0))
    lrow = pl.BlockSpec((tm, LANES), lambda i: (i, 0))
    expand = (jnp.arange(LANES)[:, None] == jnp.arange(he)[None, :] // DIL_HEAD_DIM).astype(BF16)
    out = pl.pallas_call(
        functools.partial(_dil_out_kernel, alpha=alpha),
        grid=(n // tm,),
        in_specs=[row] * 3 + [lrow] * 3 + [pl.BlockSpec((tm, d), lambda i: (i, 0)),
                                           _full((LANES, he)),
                                           _resident((he, d)), _full((1, d)), _full((1, d))],
        out_specs=pl.BlockSpec((tm, d), lambda i: (i, 0)),
        out_shape=jax.ShapeDtypeStruct((n, d), F32),
        compiler_params=_cparams("parallel"),
        name="dil_combine_out",
    )(*outs, *lses, x.reshape(n, d), expand, w_out, _row(g), _row(b))
    return out.reshape(bsz, s, d)


def _rwkv_prep_kernel(x_ref, xh_ref, mu_ref, wrkv_ref, w0_ref, w1_ref, w2_ref, a0_ref, a1_ref,
                      a2_ref, g1_ref, g2_ref, kk_ref, ka_ref,
                      r_o, k_o, v_o, lw_o, kk_o, b_o, g_o):
    i = pl.program_id(1)
    x = x_ref[0]
    tm = x.shape[0]
    prev_row = jnp.where(i == 0, 0.0, xh_ref[0, SUBLANES - 1:SUBLANES, :])
    rows = lax.broadcasted_iota(jnp.int32, (tm, 1), 0)
    xprev = jnp.where(rows == 0, prev_row, pltpu.roll(x, 1, 0))
    xx = xprev - x

    def mix(j):
        return x + xx * mu_ref[j:j + 1, :]

    r = _mm(mix(0), wrkv_ref[0])
    k = _mm(mix(1), wrkv_ref[1])
    v = _mm(mix(2), wrkv_ref[2])
    wl = w0_ref[...] + _mm(jnp.tanh(_mm(mix(3), w1_ref[...])), w2_ref[...])
    w_log = -_softplus(-wl) - 0.5
    a = _sigmoid(a0_ref[...] + _mm(_mm(mix(4), a1_ref[...]), a2_ref[...]))
    g = _mm(_sigmoid(_mm(mix(5), g1_ref[...])), g2_ref[...])
    kk = k * kk_ref[...]
    r_o[0] = r
    k_o[0] = k * (1.0 + (a - 1.0) * ka_ref[...])
    v_o[0] = v
    lw_o[0] = -jnp.exp(w_log)
    kk_o[0] = kk
    b_o[0] = kk * a
    g_o[0] = g


def _tri_inv(mats, blk):
    n = mats[0].shape[0]
    r = lax.broadcasted_iota(jnp.int32, (n, n), 0)
    c = lax.broadcasted_iota(jnp.int32, (n, n), 1)

    def same(bs):
        sh = int(math.log2(bs))
        return lax.shift_right_logical(r, sh) == lax.shift_right_logical(c, sh)

    base = same(TRI_BASE)
    eye = jnp.where(r == c, 1.0, 0.0)
    ps = [jnp.where(base, a, 0.0) for a in mats]
    ts = [eye + p for p in ps]
    m = 1
    while 2 * m < TRI_BASE:
        ps = [_mm(p, p) for p in ps]
        ts = [t + _mm(p, t) for p, t in zip(ps, ts)]
        m *= 2
    bs = TRI_BASE
    while bs < blk:
        sel = same(2 * bs) & jnp.logical_not(same(bs))
        lo = [_mm(t, jnp.where(sel, a, 0.0)) for t, a in zip(ts, mats)]
        ts = [t + _mm(l, t) for l, t in zip(lo, ts)]
        bs *= 2
    return ts


def _rwkv_scan_kernel(r_ref, k_ref, v_ref, lw_ref, kk_ref, b_ref, lg_ref, lb_ref, rk_ref,
                      y_ref, st_ref, *, chunk):
    @pl.when(pl.program_id(2) == 0)
    def _():
        st_ref[...] = jnp.zeros_like(st_ref)

    L = chunk
    n2 = 2 * L
    tb = r_ref.shape[1]
    chunks = range(tb // L)
    sh = int(math.log2(L))
    lane = lax.broadcasted_iota(jnp.int32, (L, LANES), 1)
    head0 = lane < RWKV_HEAD
    rr = lax.broadcasted_iota(jnp.int32, (n2, n2), 0)
    cc = lax.broadcasted_iota(jnp.int32, (n2, n2), 1)
    rt, ct = rr & (L - 1), cc & (L - 1)
    strict, incl, eye = rt > ct, rt >= ct, rr == cc
    tr = lax.broadcasted_iota(jnp.int32, (tb, tb), 0)
    tc = lax.broadcasted_iota(jnp.int32, (tb, tb), 1)
    same_chunk = lax.shift_right_logical(tr, sh) == lax.shift_right_logical(tc, sh)

    def indicator(cond):
        return jnp.where(cond, 1.0, 0.0).astype(BF16)

    head0_tb = lax.broadcasted_iota(jnp.int32, (tb, LANES), 1) < RWKV_HEAD

    def head_sum(z):
        s0 = jnp.sum(jnp.where(head0_tb, z, 0.0), axis=-1, keepdims=True)
        s1 = jnp.sum(jnp.where(head0_tb, 0.0, z), axis=-1, keepdims=True)
        return jnp.where(head0_tb, s0, s1)

    r, k, v = r_ref[0], k_ref[0], v_ref[0]
    lw, kk = lw_ref[0], kk_ref[0]
    inv_norm = 1.0 / jnp.maximum(jnp.sqrt(head_sum(kk * kk)), 1e-12)
    kn = kk * inv_norm
    b = b_ref[0] * inv_norm
    c = _mm_exact_lhs(indicator(same_chunk & (tr >= tc)), lw)
    c_end = jnp.concatenate(
        [jnp.broadcast_to(c[(ch + 1) * L - 1:(ch + 1) * L, :], (L, LANES)) for ch in chunks], axis=0)
    e_inv = jnp.exp(-c)
    e_end = jnp.exp(c_end - c)
    p_end = jnp.exp(c_end)
    a_t = -kn * jnp.exp(c - lw)
    r_t = r * jnp.exp(c)
    bh, kh = b * e_inv, k * e_inv
    bb, kb = b * e_end, k * e_end

    def stack(z, ch):
        z = z[ch * L:(ch + 1) * L]
        return jnp.concatenate([jnp.where(head0, z, 0.0), jnp.where(head0, 0.0, z)], axis=0)

    a_s = [stack(a_t, ch) for ch in chunks]
    r_s = [stack(r_t, ch) for ch in chunks]
    v_s = [stack(v, ch) for ch in chunks]
    ms = [_mm_nt(jnp.concatenate([a_s[ch], r_s[ch]], axis=0),
                 jnp.concatenate([stack(bh, ch), stack(kh, ch)], axis=0)) for ch in chunks]
    ts = _tri_inv([jnp.where(strict, m[:n2, :n2], 0.0) for m in ms], L)
    avbv = [_mm(jnp.concatenate([jnp.where(strict, m[:n2, n2:], 0.0),
                                 jnp.where(incl, m[n2:, n2:], 0.0)], axis=0), vs)
            for m, vs in zip(ms, v_s)]
    tu = [_mm(t, jnp.concatenate([a, av[:n2]], axis=1)) for t, a, av in zip(ts, a_s, avbv)]
    bu = [_mm(jnp.where(incl, m[n2:, :n2], 0.0), x) for m, x in zip(ms, tu)]
    gh = [_mm_tn(stack(bb, ch), tu[ch]) for ch in chunks]
    kv = [_mm_tn(stack(kb, ch), v_s[ch]) for ch in chunks]
    rg_lhs = [jnp.concatenate(
        [r_s[ch] + bu[ch][:, :LANES],
         gh[ch][:, :LANES] + jnp.where(eye, p_end[ch * L:ch * L + 1, :], 0.0)], axis=0)
        for ch in chunks]
    y0 = [bu[ch][:, LANES:] + avbv[ch][n2:] for ch in chunks]
    h0 = [gh[ch][:, LANES:] + kv[ch] for ch in chunks]

    st = st_ref[...]
    ys = []
    for ch in chunks:
        rg = _mm(rg_lhs[ch], st)
        y2 = rg[:n2] + y0[ch]
        st = rg[n2:] + h0[ch]
        ys.append(y2[:L] + y2[L:])
    st_ref[...] = st

    y = jnp.concatenate(ys, axis=0)
    inv_n = 1.0 / RWKV_HEAD
    yc = y - head_sum(y) * inv_n
    var = head_sum(yc * yc) * inv_n
    yn = yc * lax.rsqrt(var + RWKV_GN_EPS) * lg_ref[...] + lb_ref[...]
    y_ref[0] = yn + head_sum(r * k * rk_ref[...]) * v


def _rwkv_post_kernel(y_ref, gate_ref, x_ref, wo_ref, g_ref, b_ref, o_ref, *, alpha):
    hout = _mm(y_ref[...] * gate_ref[...], wo_ref[...])
    o_ref[...] = _postnorm(alpha, x_ref[...], hout, g_ref[...], b_ref[...])


def _rwkv_mixer(x, mu, w_rkv, w0, w1, w2, a0, a1, a2, g1, g2, k_k, k_a, r_k, lnx_g, lnx_b,
                w_out, g, b, alpha, tm=256, tblk=512):
    bsz, s, d = x.shape
    nh = d // RWKV_HEAD
    tm = min(tm, s)
    tile = pl.BlockSpec((1, tm, d), lambda bi, i: (bi, i, 0))
    hb = tm // SUBLANES
    lora = w1.shape[1]
    glora = g1.shape[1]
    outs = pl.pallas_call(
        _rwkv_prep_kernel,
        grid=(bsz, s // tm),
        in_specs=[tile,
                  pl.BlockSpec((1, SUBLANES, d), lambda bi, i: (bi, jnp.maximum(i * hb - 1, 0), 0)),
                  _full((6, d)), _resident((3, d, d)), _full((1, d)),
                  _full((d, lora)), _full((lora, d)), _full((1, d)),
                  _full((d, a1.shape[1])), _full((a2.shape[0], d)),
                  _full((d, glora)), _full((glora, d)),
                  _full((1, d)), _full((1, d))],
        out_specs=[tile] * 7,
        out_shape=[jax.ShapeDtypeStruct((bsz, s, d), F32)] * 7,
        compiler_params=_cparams("parallel", "parallel"),
        name="rwkv_prep",
    )(x, x, mu, w_rkv, _row(w0), w1, w2, _row(a0), a1, a2, g1, g2, _row(k_k), _row(k_a))
    r, k, v, lw, kk, bvec, gate = outs

    tblk = min(tblk, s)
    blk = pl.BlockSpec((1, tblk, LANES), lambda bi, hp, i: (bi, i, hp))
    chan = pl.BlockSpec((1, LANES), lambda bi, hp, i: (0, hp))
    y = pl.pallas_call(
        functools.partial(_rwkv_scan_kernel, chunk=RWKV_CHUNK),
        grid=(bsz, nh // 2, s // tblk),
        in_specs=[blk] * 6 + [chan] * 3,
        out_specs=blk,
        out_shape=jax.ShapeDtypeStruct((bsz, s, d), F32),
        scratch_shapes=[pltpu.VMEM((LANES, LANES), F32)],
        compiler_params=_cparams("parallel", "parallel", "arbitrary"),
        name="rwkv_scan",
    )(r, k, v, lw, kk, bvec, _row(lnx_g), _row(lnx_b), _row(r_k))

    n = bsz * s
    tm2 = min(tm, n)
    row = pl.BlockSpec((tm2, d), lambda i: (i, 0))
    flat = lambda z: z.reshape(n, d)
    out = pl.pallas_call(
        functools.partial(_rwkv_post_kernel, alpha=alpha),
        grid=(n // tm2,),
        in_specs=[row] * 3 + [_resident((d, d)), _full((1, d)), _full((1, d))],
        out_specs=row,
        out_shape=jax.ShapeDtypeStruct((n, d), F32),
        compiler_params=_cparams("parallel"),
        name="rwkv_post",
    )(flat(y), flat(gate), flat(x), w_out, _row(g), _row(b))
    return out.reshape(bsz, s, d)


def kernel(x, mem, rel_bias, a_w_in, a_b_in, a_dw, a_dw_b, a_ln_g, a_ln_b, a_w_out, a_b_out, b_w_qkv, b_w_out, c_mu, c_w_rkv, c_w0, c_w1, c_w2, c_a0, c_a1, c_a2, c_g1, c_g2, c_k_k, c_k_a, c_r_k, c_lnx_g, c_lnx_b, c_w_out, x_w_q, x_w_kv, x_w_out, m_w1, m_w2, ln_g, ln_b):
    depth = ln_g.shape[0]
    alpha = (2 * depth) ** 0.25
    bsz, s, d = x.shape
    mlen = mem.shape[1]
    w = lambda t: t.astype(MXU_DTYPE)
    mem2d = mem.reshape(bsz * mlen, d)
    for i in range(depth):
        kind, j = i % 3, i // 3
        if kind == 0:
            x = _conv_mixer(x, w(a_w_in[j]), a_b_in[j], a_dw[j], a_dw_b[j], a_ln_g[j], a_ln_b[j],
                            w(a_w_out[j]), a_b_out[j], ln_g[i, 0], ln_b[i, 0], alpha)
        elif kind == 1:
            x = _dil_mixer(x, w(b_w_qkv[j]), w(b_w_out[j]), rel_bias, ln_g[i, 0], ln_b[i, 0], alpha)
        else:
            x = _rwkv_mixer(x, c_mu[j], w(c_w_rkv[j]), c_w0[j], w(c_w1[j]), w(c_w2[j]), c_a0[j],
                            w(c_a1[j]), w(c_a2[j]), w(c_g1[j]), w(c_g2[j]), c_k_k[j], c_k_a[j],
                            c_r_k[j], c_lnx_g[j], c_lnx_b[j], w(c_w_out[j]),
                            ln_g[i, 0], ln_b[i, 0], alpha)
        kv = _linear(mem2d, w(x_w_kv[i]), MXU_DTYPE, 512, 1024).reshape(bsz, mlen, 2 * d)
        x = _cross_attention(x, kv, w(x_w_q[i]), w(x_w_out[i]), ln_g[i, 1], ln_b[i, 1], alpha)
        x = _mlp(x.reshape(bsz * s, d), w(m_w1[i]), w(m_w2[i]), ln_g[i, 2], ln_b[i, 2],
                 alpha).reshape(bsz, s, d)
    return x
```

```python
import functools
import math

import jax
import jax.numpy as jnp
from jax import lax
from jax.experimental import pallas as pl
from jax.experimental.pallas import tpu as pltpu

F32 = jnp.float32
BF16 = jnp.bfloat16
MXU_DTYPE = jnp.bfloat16

CONV_WIDTH = 31
DIL_GROUPS = ((128, 1), (512, 4), (2048, 16))
DIL_HEADS = 16
DIL_HEAD_DIM = 64
REL_BUCKETS = 32
REL_MAX_DIST = 2048
RWKV_HEAD = 64
RWKV_GN_EPS = 64e-5
XATTN_HEADS = 4
LN_EPS = 1e-5
NEG_BIG = -1e30

LANES = 128
SUBLANES = 8
VMEM_LIMIT_BYTES = 56 * 1024 * 1024

RWKV_CHUNK = 64
TRI_BASE = 8
SCAN_PAIRS = 4


def _cparams(*sem):
    return pltpu.CompilerParams(dimension_semantics=sem, vmem_limit_bytes=VMEM_LIMIT_BYTES)


def _mm(a, b):
    return jnp.dot(a.astype(MXU_DTYPE), b.astype(MXU_DTYPE), preferred_element_type=F32)


def _mm_nt(a, b):
    return lax.dot_general(a.astype(MXU_DTYPE), b.astype(MXU_DTYPE),
                           (((1,), (1,)), ((), ())), preferred_element_type=F32)


def _mm_tn(a, b):
    return _mm(a.T, b)


def _split3(x):
    hi = x.astype(BF16)
    r1 = x - hi.astype(F32)
    mid = r1.astype(BF16)
    lo = (r1 - mid.astype(F32)).astype(BF16)
    return hi, mid, lo


def _mm_exact_rhs(x, e):
    hi, mid, lo = _split3(x)
    dot = functools.partial(jnp.dot, preferred_element_type=F32)
    return dot(hi, e) + dot(mid, e) + dot(lo, e)


def _mm_exact_lhs(e, x):
    hi, mid, lo = _split3(x)
    dot = functools.partial(jnp.dot, preferred_element_type=F32)
    return dot(e, hi) + dot(e, mid) + dot(e, lo)


def _sigmoid(z):
    return 1.0 / (1.0 + jnp.exp(-z))


def _softplus(z):
    return jnp.maximum(z, 0.0) + jnp.log(1.0 + jnp.exp(-jnp.abs(z)))


def _layer_norm(z, g, b, eps=LN_EPS):
    mu = jnp.mean(z, axis=-1, keepdims=True)
    zc = z - mu
    var = jnp.mean(zc * zc, axis=-1, keepdims=True)
    return zc * lax.rsqrt(var + eps) * g + b


def _postnorm(alpha, x, h, g, b):
    return _layer_norm(alpha * x + h, g, b)


def _full(shape):
    nd = len(shape)
    return pl.BlockSpec(shape, lambda *_: (0,) * nd)


def _resident(shape):
    nd = len(shape)
    return pl.BlockSpec(shape, lambda *_: (0,) * nd, pipeline_mode=pl.Buffered(1))


def _row(v):
    return v.reshape(1, -1)


def _linear_kernel(x_ref, w_ref, o_ref):
    o_ref[...] = _mm(x_ref[...], w_ref[...]).astype(o_ref.dtype)


def _linear(x2d, w, out_dtype, tm, tn):
    m, k = x2d.shape
    n = w.shape[1]
    tm, tn = min(tm, m), min(tn, n)
    return pl.pallas_call(
        _linear_kernel,
        grid=(m // tm, n // tn),
        in_specs=[pl.BlockSpec((tm, k), lambda i, j: (i, 0)),
                  pl.BlockSpec((k, tn), lambda i, j: (0, j))],
        out_specs=pl.BlockSpec((tm, tn), lambda i, j: (i, j)),
        out_shape=jax.ShapeDtypeStruct((m, n), out_dtype),
        compiler_params=_cparams("parallel", "parallel"),
        name="linear",
    )(x2d, w)


def _xattn_kernel(x_ref, kv_ref, wq_ref, wo_ref, g_ref, b_ref, o_ref, oh_ref, *, heads, alpha):
    xb = x_ref[0]
    d = xb.shape[-1]
    e = d // heads
    q = (_mm(xb, wq_ref[...]) * (e ** -0.5)).astype(MXU_DTYPE)
    for h in range(heads):
        kh = kv_ref[0, :, h * e:(h + 1) * e]
        vh = kv_ref[0, :, d + h * e:d + (h + 1) * e]
        s = _mm_nt(q[:, h * e:(h + 1) * e], kh)
        m = jnp.max(s, axis=-1, keepdims=True)
        p = jnp.exp(s - m)
        den = jnp.sum(p, axis=-1, keepdims=True)
        oh_ref[:, h * e:(h + 1) * e] = _mm(p, vh) / den
    hout = _mm(oh_ref[...], wo_ref[...])
    o_ref[0] = _postnorm(alpha, xb, hout, g_ref[...], b_ref[...])


def _cross_attention(x, kv, wq, wo, g, b, alpha, tm=512):
    bsz, s, d = x.shape
    mlen = kv.shape[1]
    tm = min(tm, s)
    return pl.pallas_call(
        functools.partial(_xattn_kernel, heads=XATTN_HEADS, alpha=alpha),
        grid=(bsz, s // tm),
        in_specs=[pl.BlockSpec((1, tm, d), lambda bi, i: (bi, i, 0)),
                  pl.BlockSpec((1, mlen, 2 * d), lambda bi, i: (bi, 0, 0)),
                  _resident((d, d)), _resident((d, d)), _full((1, d)), _full((1, d))],
        out_specs=pl.BlockSpec((1, tm, d), lambda bi, i: (bi, i, 0)),
        out_shape=jax.ShapeDtypeStruct((bsz, s, d), F32),
        scratch_shapes=[pltpu.VMEM((tm, d), F32)],
        compiler_params=_cparams("parallel", "parallel"),
        name="cross_attention",
    )(x, kv, wq, wo, _row(g), _row(b))


def _mlp_kernel(x_ref, w1_ref, w2_ref, g_ref, b_ref, o_ref, *, fchunk, alpha):
    xb = x_ref[...]
    xm = xb.astype(MXU_DTYPE)
    ff = w1_ref.shape[1]
    acc = jnp.zeros(xb.shape, F32)
    for c in range(ff // fchunk):
        hid = jnp.dot(xm, w1_ref[:, c * fchunk:(c + 1) * fchunk], preferred_element_type=F32)
        hid = jnp.maximum(hid, 0.0)
        acc = acc + _mm(hid * hid, w2_ref[c * fchunk:(c + 1) * fchunk, :])
    o_ref[...] = _postnorm(alpha, xb, acc, g_ref[...], b_ref[...])


def _mlp(x2d, w1, w2, g, b, alpha, tm=512, fchunk=1024):
    n, d = x2d.shape
    ff = w1.shape[1]
    tm = min(tm, n)
    return pl.pallas_call(
        functools.partial(_mlp_kernel, fchunk=min(fchunk, ff), alpha=alpha),
        grid=(n // tm,),
        in_specs=[pl.BlockSpec((tm, d), lambda i: (i, 0)),
                  _resident((d, ff)), _resident((ff, d)), _full((1, d)), _full((1, d))],
        out_specs=pl.BlockSpec((tm, d), lambda i: (i, 0)),
        out_shape=jax.ShapeDtypeStruct((n, d), F32),
        compiler_params=_cparams("parallel"),
        name="sq_relu_mlp",
    )(x2d, w1, w2, _row(g), _row(b))


def _glu_kernel(x_ref, w_ref, b_ref, o_ref):
    h = _mm(x_ref[...], w_ref[...]) + b_ref[...]
    d = o_ref.shape[-1]
    o_ref[...] = h[:, :d] * _sigmoid(h[:, d:])


def _glu_proj(x2d, w, b, tm=512):
    n, d = x2d.shape
    tm = min(tm, n)
    return pl.pallas_call(
        _glu_kernel,
        grid=(n // tm,),
        in_specs=[pl.BlockSpec((tm, d), lambda i: (i, 0)), _resident((d, 2 * d)), _full((1, 2 * d))],
        out_specs=pl.BlockSpec((tm, d), lambda i: (i, 0)),
        out_shape=jax.ShapeDtypeStruct((n, d), F32),
        compiler_params=_cparams("parallel"),
        name="conv_glu_proj",
    )(x2d, w, _row(b))


CONV_HALO = 32


def _conv_kernel(u_ref, halo_ref, x_ref, dw_ref, dwb_ref, lng_ref, lnb_ref, wo_ref, bo_ref,
                 g_ref, b_ref, o_ref, cat_ref, sh_ref, *, alpha):
    i = pl.program_id(1)
    tm = u_ref.shape[1]
    cat_ref[0:CONV_HALO, :] = jnp.where(i == 0, 0.0, halo_ref[0])
    cat_ref[CONV_HALO:CONV_HALO + tm, :] = u_ref[0]
    acc = jnp.broadcast_to(dwb_ref[...], (tm, u_ref.shape[2]))
    off = CONV_HALO - (CONV_WIDTH - 1)
    for phase in range(SUBLANES):
        taps = [j for j in range(CONV_WIDTH) if (off + j) % SUBLANES == phase]
        span = max(off + j for j in taps) - phase + tm
        src = cat_ref
        if phase:
            sh_ref[0:span, :] = cat_ref[phase:phase + span, :]
            src = sh_ref
        for j in taps:
            a = off + j - phase
            acc = acc + dw_ref[j:j + 1, :] * src[a:a + tm, :]
    z = _layer_norm(acc, lng_ref[...], lnb_ref[...])
    z = z * _sigmoid(z)
    hout = _mm(z, wo_ref[...]) + bo_ref[...]
    o_ref[0] = _postnorm(alpha, x_ref[0], hout, g_ref[...], b_ref[...])


def _conv_mixer(x, w_in, b_in, dw, dw_b, ln_g, ln_b, w_out, b_out, g, b, alpha, tm=256):
    bsz, s, d = x.shape
    u = _glu_proj(x.reshape(bsz * s, d), w_in, b_in).reshape(bsz, s, d)
    tm = min(tm, s)
    hb = tm // CONV_HALO
    return pl.pallas_call(
        functools.partial(_conv_kernel, alpha=alpha),
        grid=(bsz, s // tm),
        in_specs=[pl.BlockSpec((1, tm, d), lambda bi, i: (bi, i, 0)),
                  pl.BlockSpec((1, CONV_HALO, d), lambda bi, i: (bi, jnp.maximum(i * hb - 1, 0), 0)),
                  pl.BlockSpec((1, tm, d), lambda bi, i: (bi, i, 0)),
                  _full((CONV_WIDTH, d)), _full((1, d)), _full((1, d)), _full((1, d)),
                  _resident((d, d)), _full((1, d)), _full((1, d)), _full((1, d))],
        out_specs=pl.BlockSpec((1, tm, d), lambda bi, i: (bi, i, 0)),
        out_shape=jax.ShapeDtypeStruct((bsz, s, d), F32),
        scratch_shapes=[pltpu.VMEM((CONV_HALO + tm, d), F32), pltpu.VMEM((CONV_HALO + tm, d), F32)],
        compiler_params=_cparams("parallel", "parallel"),
        name="conv_dw_ln_out",
    )(u, u, x, dw, _row(dw_b), _row(ln_g), _row(ln_b), w_out, _row(b_out), _row(g), _row(b))


def _t5_causal_bucket(dist):
    n = jnp.maximum(dist, 0)
    max_exact = REL_BUCKETS // 2
    nf = jnp.maximum(n, 1).astype(F32)
    large = max_exact + (jnp.log(nf / max_exact) / math.log(REL_MAX_DIST / max_exact)
                         * (REL_BUCKETS - max_exact)).astype(jnp.int32)
    large = jnp.minimum(large, REL_BUCKETS - 1)
    return jnp.where(n < max_exact, n, large)


def _qkv_kernel(x_ref, w_ref, o_ref, xs_ref, xp_ref, *, dil, qcols, qscale):
    tm = x_ref.shape[1]
    n = tm // dil
    if dil == 1:
        xp = x_ref[0]
    else:
        for j in range(xs_ref.shape[0]):
            cols = slice(j * LANES, (j + 1) * LANES)
            xs_ref[j] = x_ref[0, :, cols]
            for c in range(dil):
                xp_ref[c * n:(c + 1) * n, cols] = (
                    xs_ref[j, pl.ds(c, n, stride=dil), :].astype(xp_ref.dtype))
        xp = xp_ref[...]
    acc = _mm(xp, w_ref[...])
    for c in range(dil):
        o_ref[0, c, :, :qcols] = (acc[c * n:(c + 1) * n, :qcols] * qscale).astype(o_ref.dtype)
        o_ref[0, c, :, qcols:] = acc[c * n:(c + 1) * n, qcols:].astype(o_ref.dtype)


def _qkv_proj(x, w_g, dil, tm=512):
    bsz, s, d = x.shape
    sub = s // dil
    ncol = w_g.shape[1]
    tm = min(tm, s)
    return pl.pallas_call(
        functools.partial(_qkv_kernel, dil=dil, qcols=ncol // 3, qscale=DIL_HEAD_DIM ** -0.5),
        grid=(bsz, s // tm),
        in_specs=[pl.BlockSpec((1, tm, d), lambda bi, i: (bi, i, 0)),
                  _resident((d, ncol))],
        out_specs=pl.BlockSpec((1, dil, tm // dil, ncol), lambda bi, i: (bi, 0, i, 0)),
        out_shape=jax.ShapeDtypeStruct((bsz, dil, sub, ncol), MXU_DTYPE),
        scratch_shapes=[pltpu.VMEM((d // LANES, tm, LANES), F32), pltpu.VMEM((tm, d), MXU_DTYPE)],
        compiler_params=_cparams("parallel", "parallel"),
        name="dil_qkv_proj",
    )(x, w_g)


def _dil_attn_kernel(bucket_ref, relb_ref, q_ref, k_ref, v_ref, kp_ref, vp_ref, o_ref, l_ref,
                     kc_ref, vc_ref, bias_ref, *, heads, blk):
    first_step = (pl.program_id(0) == 0) & (pl.program_id(1) == 0) & (pl.program_id(2) == 0)

    @pl.when(first_step)
    def _():
        bkt = bucket_ref[...]
        kcol = lax.broadcasted_iota(jnp.int32, bkt.shape, 1)
        for h in range(heads):
            acc = jnp.full(bkt.shape, NEG_BIG, F32)
            for n in range(REL_BUCKETS):
                acc = jnp.where(bkt == n, relb_ref[n, h], acc)
            bias_ref[h] = acc
            bias_ref[heads + h] = jnp.where(kcol < blk, NEG_BIG, acc)

    i = pl.program_id(2)
    tq = q_ref.shape[2]
    kc_ref[0:blk, :] = kp_ref[0, 0]
    kc_ref[blk:blk + tq, :] = k_ref[0, 0]
    vc_ref[0:blk, :] = vp_ref[0, 0]
    vc_ref[blk:blk + tq, :] = v_ref[0, 0]
    lane = lax.broadcasted_iota(jnp.int32, (blk, LANES), 1)
    low = lane < DIL_HEAD_DIM

    def body(jb, carry):
        r0 = pl.multiple_of(jb * blk, blk)
        table = jnp.where((i == 0) & (jb == 0), heads, 0)
        lse_all = jnp.zeros((blk, LANES), F32)
        for hp in range(heads // 2):
            cols = slice(hp * LANES, (hp + 1) * LANES)
            q2 = q_ref[0, 0, pl.ds(r0, blk), cols]
            k2 = kc_ref[pl.ds(r0, 2 * blk), cols]
            v2 = vc_ref[pl.ds(r0, 2 * blk), cols]
            outs = []
            for hh in range(2):
                h = 2 * hp + hh
                keep = low if hh == 0 else jnp.logical_not(low)
                qm = jnp.where(keep, q2, jnp.zeros_like(q2))
                s = _mm_nt(qm, k2) + bias_ref[table + h]
                m = jnp.max(s, axis=-1, keepdims=True)
                p = jnp.exp(s - m)
                den = jnp.sum(p, axis=-1, keepdims=True)
                outs.append(_mm(p, v2) / den)
                lse_all = jnp.where(lane == h, m + jnp.log(den), lse_all)
            o_ref[0, pl.ds(r0, blk), cols] = jnp.where(low, outs[0], outs[1]).astype(o_ref.dtype)
        l_ref[0, pl.ds(r0, blk), :] = lse_all
        return carry

    lax.fori_loop(0, tq // blk, body, 0)


def _dil_attention_group(qkv, rel_bias, window, dil, tq=512):
    bsz, _, sub, ncol = qkv.shape
    he = ncol // 3
    blk = window // dil
    tq = min(tq, sub)
    nsub = tq // blk
    qi = jnp.arange(blk)[:, None]
    kj = jnp.arange(2 * blk)[None, :]
    rel = qi + blk - kj
    bucket = jnp.where((rel >= 0) & (rel <= blk), _t5_causal_bucket(rel * dil), -1).astype(jnp.int32)

    def prev_map(col):
        return lambda bi, c, i: (bi, c, jnp.maximum(i * nsub - 1, 0), col)

    o, lse = pl.pallas_call(
        functools.partial(_dil_attn_kernel, heads=DIL_HEADS, blk=blk),
        grid=(bsz, dil, sub // tq),
        in_specs=[_full((blk, 2 * blk)),
                  pl.BlockSpec(memory_space=pltpu.SMEM),
                  pl.BlockSpec((1, 1, tq, he), lambda bi, c, i: (bi, c, i, 0)),
                  pl.BlockSpec((1, 1, tq, he), lambda bi, c, i: (bi, c, i, 1)),
                  pl.BlockSpec((1, 1, tq, he), lambda bi, c, i: (bi, c, i, 2)),
                  pl.BlockSpec((1, 1, blk, he), prev_map(1)),
                  pl.BlockSpec((1, 1, blk, he), prev_map(2))],
        out_specs=[pl.BlockSpec((1, tq, he), lambda bi, c, i: (bi, i, c)),
                   pl.BlockSpec((1, tq, LANES), lambda bi, c, i: (bi, i, c))],
        out_shape=[jax.ShapeDtypeStruct((bsz, sub, dil * he), MXU_DTYPE),
                   jax.ShapeDtypeStruct((bsz, sub, dil * LANES), F32)],
        scratch_shapes=[pltpu.VMEM((blk + tq, he), MXU_DTYPE),
                        pltpu.VMEM((blk + tq, he), MXU_DTYPE),
                        pltpu.VMEM((2 * DIL_HEADS, blk, 2 * blk), F32)],
        compiler_params=_cparams("arbitrary", "arbitrary", "arbitrary"),
        name="dil_attention",
    )(bucket, rel_bias, qkv, qkv, qkv, qkv, qkv)
    s = sub * dil
    return o.reshape(bsz * s, he), lse.reshape(bsz * s, LANES)


def _dil_out_kernel(o1_ref, o2_ref, o3_ref, l1_ref, l2_ref, l3_ref, x_ref, e_ref, wo_ref, g_ref,
                    b_ref, out_ref, *, alpha):
    l1, l2, l3 = l1_ref[...], l2_ref[...], l3_ref[...]
    m = jnp.maximum(jnp.maximum(l1, l2), l3)
    e1, e2, e3 = jnp.exp(l1 - m), jnp.exp(l2 - m), jnp.exp(l3 - m)
    inv = 1.0 / (e1 + e2 + e3)
    o = (_mm_exact_rhs(e1 * inv, e_ref[...]) * o1_ref[...]
         + _mm_exact_rhs(e2 * inv, e_ref[...]) * o2_ref[...]
         + _mm_exact_rhs(e3 * inv, e_ref[...]) * o3_ref[...])
    hout = _mm(o, wo_ref[...])
    out_ref[...] = _postnorm(alpha, x_ref[...], hout, g_ref[...], b_ref[...])


def _dil_mixer(x, w_qkv, w_out, rel_bias, g, b, alpha, tm=256):
    bsz, s, d = x.shape
    he = DIL_HEADS * DIL_HEAD_DIM
    outs, lses = [], []
    for gi, (window, dil) in enumerate(DIL_GROUPS):
        qkv = _qkv_proj(x, w_qkv[:, gi * 3 * he:(gi + 1) * 3 * he], dil)
        o, lse = _dil_attention_group(qkv, rel_bias, window, dil)
        outs.append(o)
        lses.append(lse)
    n = bsz * s
    tm = min(tm, n)
    row = pl.BlockSpec((tm, he), lambda i: (i, 0))
    lrow = pl.BlockSpec((tm, LANES), lambda i: (i, 0))
    expand = (jnp.arange(LANES)[:, None] == jnp.arange(he)[None, :] // DIL_HEAD_DIM).astype(BF16)
    out = pl.pallas_call(
        functools.partial(_dil_out_kernel, alpha=alpha),
        grid=(n // tm,),
        in_specs=[row] * 3 + [lrow] * 3 + [pl.BlockSpec((tm, d), lambda i: (i, 0)),
                                           _full((LANES, he)),
                                           _resident((he, d)), _full((1, d)), _full((1, d))],
        out_specs=pl.BlockSpec((tm, d), lambda i: (i, 0)),
        out_shape=jax.ShapeDtypeStruct((n, d), F32),
        compiler_params=_cparams("parallel"),
        name="dil_combine_out",
    )(*outs, *lses, x.reshape(n, d), expand, w_out, _row(g), _row(b))
    return out.reshape(bsz, s, d)


def _rwkv_prep_kernel(x_ref, xh_ref, mu_ref, wrkv_ref, w0_ref, w1_ref, w2_ref, a0_ref, a1_ref,
                      a2_ref, g1_ref, g2_ref, kk_ref, ka_ref,
                      r_o, k_o, v_o, lw_o, kk_o, b_o, g_o):
    i = pl.program_id(1)
    x = x_ref[0]
    tm = x.shape[0]
    prev_row = jnp.where(i == 0, 0.0, xh_ref[0, SUBLANES - 1:SUBLANES, :])
    rows = lax.broadcasted_iota(jnp.int32, (tm, 1), 0)
    xprev = jnp.where(rows == 0, prev_row, pltpu.roll(x, 1, 0))
    xx = xprev - x

    def mix(j):
        return x + xx * mu_ref[j:j + 1, :]

    r = _mm(mix(0), wrkv_ref[0])
    k = _mm(mix(1), wrkv_ref[1])
    v = _mm(mix(2), wrkv_ref[2])
    wl = w0_ref[...] + _mm(jnp.tanh(_mm(mix(3), w1_ref[...])), w2_ref[...])
    w_log = -_softplus(-wl) - 0.5
    a = _sigmoid(a0_ref[...] + _mm(_mm(mix(4), a1_ref[...]), a2_ref[...]))
    g = _mm(_sigmoid(_mm(mix(5), g1_ref[...])), g2_ref[...])
    kk = k * kk_ref[...]
    r_o[0] = r
    k_o[0] = k * (1.0 + (a - 1.0) * ka_ref[...])
    v_o[0] = v
    lw_o[0] = -jnp.exp(w_log)
    kk_o[0] = kk
    b_o[0] = kk * a
    g_o[0] = g


def _tri_inv(mats, blk):
    n = mats[0].shape[0]
    r = lax.broadcasted_iota(jnp.int32, (n, n), 0)
    c = lax.broadcasted_iota(jnp.int32, (n, n), 1)

    def same(bs):
        sh = int(math.log2(bs))
        return lax.shift_right_logical(r, sh) == lax.shift_right_logical(c, sh)

    base = same(TRI_BASE)
    eye = jnp.where(r == c, 1.0, 0.0)
    ps = [jnp.where(base, a, 0.0) for a in mats]
    ts = [eye + p for p in ps]
    ps = [_mm(p, p) for p in ps]
    m = 2
    while 2 * m < TRI_BASE:
        both = [_mm(p, jnp.concatenate([p, t], axis=1)) for p, t in zip(ps, ts)]
        ts = [t + x[:, n:] for t, x in zip(ts, both)]
        ps = [x[:, :n] for x in both]
        m *= 2
    ts = [t + _mm(p, t) for p, t in zip(ps, ts)]
    bs = TRI_BASE
    while bs < blk:
        sel = same(2 * bs) & jnp.logical_not(same(bs))
        lo = [_mm(t, jnp.where(sel, a, 0.0)) for t, a in zip(ts, mats)]
        ts = [t + _mm(l, t) for l, t in zip(lo, ts)]
        bs *= 2
    return ts


def _rwkv_scan_kernel(r_ref, k_ref, v_ref, lw_ref, kk_ref, b_ref, lg_ref, lb_ref, rk_ref,
                      y_ref, st_ref, *, chunk):
    @pl.when(pl.program_id(2) == 0)
    def _():
        st_ref[...] = jnp.zeros_like(st_ref)

    L = chunk
    n2 = 2 * L
    tb, width = r_ref.shape[1], r_ref.shape[2]
    chunks = range(tb // L)
    pairs = range(width // LANES)
    probs = [(p, ch) for ch in chunks for p in pairs]
    sh = int(math.log2(L))
    lane = lax.broadcasted_iota(jnp.int32, (L, LANES), 1)
    head0 = lane < RWKV_HEAD
    rr = lax.broadcasted_iota(jnp.int32, (n2, n2), 0)
    cc = lax.broadcasted_iota(jnp.int32, (n2, n2), 1)
    rt, ct = rr & (L - 1), cc & (L - 1)
    strict, incl, eye = rt > ct, rt >= ct, rr == cc
    tr = lax.broadcasted_iota(jnp.int32, (tb, tb), 0)
    tc = lax.broadcasted_iota(jnp.int32, (tb, tb), 1)
    same_chunk = lax.shift_right_logical(tr, sh) == lax.shift_right_logical(tc, sh)
    head0_tb = lax.broadcasted_iota(jnp.int32, (tb, LANES), 1) < RWKV_HEAD

    def head_sum(z):
        out = []
        for p in pairs:
            zp = z[:, p * LANES:(p + 1) * LANES]
            s0 = jnp.sum(jnp.where(head0_tb, zp, 0.0), axis=-1, keepdims=True)
            s1 = jnp.sum(jnp.where(head0_tb, 0.0, zp), axis=-1, keepdims=True)
            out.append(jnp.where(head0_tb, s0, s1))
        return jnp.concatenate(out, axis=1)

    r, k, v = r_ref[0], k_ref[0], v_ref[0]
    lw, kk = lw_ref[0], kk_ref[0]
    inv_norm = 1.0 / jnp.maximum(jnp.sqrt(head_sum(kk * kk)), 1e-12)
    kn = kk * inv_norm
    b = b_ref[0] * inv_norm
    tri = jnp.where(same_chunk & (tr >= tc), 1.0, 0.0).astype(BF16)
    c = _mm_exact_lhs(tri, lw)
    c_end = jnp.concatenate(
        [jnp.broadcast_to(c[(ch + 1) * L - 1:(ch + 1) * L, :], (L, width)) for ch in chunks], axis=0)
    e_inv = jnp.exp(-c)
    e_end = jnp.exp(c_end - c)
    p_end = jnp.exp(c_end)
    a_t = -kn * jnp.exp(c - lw)
    r_t = r * jnp.exp(c)
    bh, kh = b * e_inv, k * e_inv
    bb, kb = b * e_end, k * e_end

    def stack(z, prob):
        p, ch = prob
        z = z[ch * L:(ch + 1) * L, p * LANES:(p + 1) * LANES]
        return jnp.concatenate([jnp.where(head0, z, 0.0), jnp.where(head0, 0.0, z)], axis=0)

    a_s = [stack(a_t, q) for q in probs]
    r_s = [stack(r_t, q) for q in probs]
    v_s = [stack(v, q) for q in probs]
    ms = [_mm_nt(jnp.concatenate([a, rs], axis=0),
                 jnp.concatenate([stack(bh, q), stack(kh, q)], axis=0))
          for a, rs, q in zip(a_s, r_s, probs)]
    ts = _tri_inv([jnp.where(strict, m[:n2, :n2], 0.0) for m in ms], L)
    avbv = [_mm(jnp.concatenate([jnp.where(strict, m[:n2, n2:], 0.0),
                                 jnp.where(incl, m[n2:, n2:], 0.0)], axis=0), vs)
            for m, vs in zip(ms, v_s)]
    tu = [_mm(t, jnp.concatenate([a, av[:n2]], axis=1)) for t, a, av in zip(ts, a_s, avbv)]
    bu = [_mm(jnp.where(incl, m[n2:, :n2], 0.0), x) for m, x in zip(ms, tu)]
    gh = [_mm_tn(stack(bb, q), x) for q, x in zip(probs, tu)]
    kv = [_mm_tn(stack(kb, q), vs) for q, vs in zip(probs, v_s)]
    rg_lhs, y0, h0 = {}, {}, {}
    for i, (p, ch) in enumerate(probs):
        p_last = p_end[ch * L:ch * L + 1, p * LANES:(p + 1) * LANES]
        rg_lhs[p, ch] = jnp.concatenate(
            [r_s[i] + bu[i][:, :LANES], gh[i][:, :LANES] + jnp.where(eye, p_last, 0.0)], axis=0)
        y0[p, ch] = bu[i][:, LANES:] + avbv[i][n2:]
        h0[p, ch] = gh[i][:, LANES:] + kv[i]

    st = [st_ref[p] for p in pairs]
    ys = [[] for _ in pairs]
    for ch in chunks:
        for p in pairs:
            rg = _mm(rg_lhs[p, ch], st[p])
            y2 = rg[:n2] + y0[p, ch]
            st[p] = rg[n2:] + h0[p, ch]
            ys[p].append(y2[:L] + y2[L:])
    for p in pairs:
        st_ref[p] = st[p]

    y = jnp.concatenate([jnp.concatenate(yp, axis=0) for yp in ys], axis=1)
    inv_n = 1.0 / RWKV_HEAD
    yc = y - head_sum(y) * inv_n
    var = head_sum(yc * yc) * inv_n
    yn = yc * lax.rsqrt(var + RWKV_GN_EPS) * lg_ref[...] + lb_ref[...]
    y_ref[0] = yn + head_sum(r * k * rk_ref[...]) * v


def _rwkv_post_kernel(y_ref, gate_ref, x_ref, wo_ref, g_ref, b_ref, o_ref, *, alpha):
    hout = _mm(y_ref[...] * gate_ref[...], wo_ref[...])
    o_ref[...] = _postnorm(alpha, x_ref[...], hout, g_ref[...], b_ref[...])


def _rwkv_mixer(x, mu, w_rkv, w0, w1, w2, a0, a1, a2, g1, g2, k_k, k_a, r_k, lnx_g, lnx_b,
                w_out, g, b, alpha, tm=256, tblk=256):
    bsz, s, d = x.shape
    tm = min(tm, s)
    tile = pl.BlockSpec((1, tm, d), lambda bi, i: (bi, i, 0))
    hb = tm // SUBLANES
    lora = w1.shape[1]
    glora = g1.shape[1]
    outs = pl.pallas_call(
        _rwkv_prep_kernel,
        grid=(bsz, s // tm),
        in_specs=[tile,
                  pl.BlockSpec((1, SUBLANES, d), lambda bi, i: (bi, jnp.maximum(i * hb - 1, 0), 0)),
                  _full((6, d)), _resident((3, d, d)), _full((1, d)),
                  _full((d, lora)), _full((lora, d)), _full((1, d)),
                  _full((d, a1.shape[1])), _full((a2.shape[0], d)),
                  _full((d, glora)), _full((glora, d)),
                  _full((1, d)), _full((1, d))],
        out_specs=[tile] * 7,
        out_shape=[jax.ShapeDtypeStruct((bsz, s, d), F32)] * 7,
        compiler_params=_cparams("parallel", "parallel"),
        name="rwkv_prep",
    )(x, x, mu, w_rkv, _row(w0), w1, w2, _row(a0), a1, a2, g1, g2, _row(k_k), _row(k_a))
    r, k, v, lw, kk, bvec, gate = outs

    tblk = min(tblk, s)
    width = SCAN_PAIRS * LANES
    blk = pl.BlockSpec((1, tblk, width), lambda bi, hp, i: (bi, i, hp))
    chan = pl.BlockSpec((1, width), lambda bi, hp, i: (0, hp))
    y = pl.pallas_call(
        functools.partial(_rwkv_scan_kernel, chunk=RWKV_CHUNK),
        grid=(bsz, d // width, s // tblk),
        in_specs=[blk] * 6 + [chan] * 3,
        out_specs=blk,
        out_shape=jax.ShapeDtypeStruct((bsz, s, d), F32),
        scratch_shapes=[pltpu.VMEM((SCAN_PAIRS, LANES, LANES), F32)],
        compiler_params=_cparams("parallel", "parallel", "arbitrary"),
        name="rwkv_scan",
    )(r, k, v, lw, kk, bvec, _row(lnx_g), _row(lnx_b), _row(r_k))

    n = bsz * s
    tm2 = min(tm, n)
    row = pl.BlockSpec((tm2, d), lambda i: (i, 0))
    flat = lambda z: z.reshape(n, d)
    out = pl.pallas_call(
        functools.partial(_rwkv_post_kernel, alpha=alpha),
        grid=(n // tm2,),
        in_specs=[row] * 3 + [_resident((d, d)), _full((1, d)), _full((1, d))],
        out_specs=row,
        out_shape=jax.ShapeDtypeStruct((n, d), F32),
        compiler_params=_cparams("parallel"),
        name="rwkv_post",
    )(flat(y), flat(gate), flat(x), w_out, _row(g), _row(b))
    return out.reshape(bsz, s, d)


def kernel(x, mem, rel_bias, a_w_in, a_b_in, a_dw, a_dw_b, a_ln_g, a_ln_b, a_w_out, a_b_out, b_w_qkv, b_w_out, c_mu, c_w_rkv, c_w0, c_w1, c_w2, c_a0, c_a1, c_a2, c_g1, c_g2, c_k_k, c_k_a, c_r_k, c_lnx_g, c_lnx_b, c_w_out, x_w_q, x_w_kv, x_w_out, m_w1, m_w2, ln_g, ln_b):
    depth = ln_g.shape[0]
    alpha = (2 * depth) ** 0.25
    bsz, s, d = x.shape
    mlen = mem.shape[1]
    w = lambda t: t.astype(MXU_DTYPE)
    mem2d = mem.reshape(bsz * mlen, d)
    for i in range(depth):
        kind, j = i % 3, i // 3
        if kind == 0:
            x = _conv_mixer(x, w(a_w_in[j]), a_b_in[j], a_dw[j], a_dw_b[j], a_ln_g[j], a_ln_b[j],
                            w(a_w_out[j]), a_b_out[j], ln_g[i, 0], ln_b[i, 0], alpha)
        elif kind == 1:
            x = _dil_mixer(x, w(b_w_qkv[j]), w(b_w_out[j]), rel_bias, ln_g[i, 0], ln_b[i, 0], alpha)
        else:
            x = _rwkv_mixer(x, c_mu[j], w(c_w_rkv[j]), c_w0[j], w(c_w1[j]), w(c_w2[j]), c_a0[j],
                            w(c_a1[j]), w(c_a2[j]), w(c_g1[j]), w(c_g2[j]), c_k_k[j], c_k_a[j],
                            c_r_k[j], c_lnx_g[j], c_lnx_b[j], w(c_w_out[j]),
                            ln_g[i, 0], ln_b[i, 0], alpha)
        kv = _linear(mem2d, w(x_w_kv[i]), MXU_DTYPE, 512, 1024).reshape(bsz, mlen, 2 * d)
        x = _cross_attention(x, kv, w(x_w_q[i]), w(x_w_out[i]), ln_g[i, 1], ln_b[i, 1], alpha)
        x = _mlp(x.reshape(bsz * s, d), w(m_w1[i]), w(m_w2[i]), ln_g[i, 2], ln_b[i, 2],
                 alpha).reshape(bsz, s, d)
    return x
```

```python
import functools
import math

import jax
import jax.numpy as jnp
from jax import lax
from jax.experimental import pallas as pl
from jax.experimental.pallas import tpu as pltpu

F32 = jnp.float32
BF16 = jnp.bfloat16
MXU_DTYPE = jnp.bfloat16

CONV_WIDTH = 31
DIL_GROUPS = ((128, 1), (512, 4), (2048, 16))
DIL_HEADS = 16
DIL_HEAD_DIM = 64
REL_BUCKETS = 32
REL_MAX_DIST = 2048
RWKV_HEAD = 64
RWKV_GN_EPS = 64e-5
XATTN_HEADS = 4
LN_EPS = 1e-5
NEG_BIG = -1e30

LANES = 128
SUBLANES = 8
VMEM_LIMIT_BYTES = 56 * 1024 * 1024

RWKV_CHUNK = 64
TRI_BASE = 8
SCAN_PAIRS = 4


def _cparams(*sem):
    return pltpu.CompilerParams(dimension_semantics=sem, vmem_limit_bytes=VMEM_LIMIT_BYTES)


def _mm(a, b):
    return jnp.dot(a.astype(MXU_DTYPE), b.astype(MXU_DTYPE), preferred_element_type=F32)


def _mm_nt(a, b):
    return lax.dot_general(a.astype(MXU_DTYPE), b.astype(MXU_DTYPE),
                           (((1,), (1,)), ((), ())), preferred_element_type=F32)


def _mm_tn(a, b):
    return _mm(a.T, b)


def _split3(x):
    hi = x.astype(BF16)
    r1 = x - hi.astype(F32)
    mid = r1.astype(BF16)
    lo = (r1 - mid.astype(F32)).astype(BF16)
    return hi, mid, lo


def _mm_exact_rhs(x, e, passes=3):
    pieces = _split3(x)[:passes]
    return sum(jnp.dot(p, e, preferred_element_type=F32) for p in pieces)


def _mm_exact_lhs(e, x):
    hi, mid, lo = _split3(x)
    dot = functools.partial(jnp.dot, preferred_element_type=F32)
    return dot(e, hi) + dot(e, mid) + dot(e, lo)


def _sigmoid(z):
    return 1.0 / (1.0 + jnp.exp(-z))


def _softplus(z):
    return jnp.maximum(z, 0.0) + jnp.log(1.0 + jnp.exp(-jnp.abs(z)))


def _layer_norm(z, g, b, eps=LN_EPS):
    mu = jnp.mean(z, axis=-1, keepdims=True)
    zc = z - mu
    var = jnp.mean(zc * zc, axis=-1, keepdims=True)
    return zc * lax.rsqrt(var + eps) * g + b


def _postnorm(alpha, x, h, g, b):
    return _layer_norm(alpha * x + h, g, b)


def _full(shape):
    nd = len(shape)
    return pl.BlockSpec(shape, lambda *_: (0,) * nd)


def _resident(shape):
    nd = len(shape)
    return pl.BlockSpec(shape, lambda *_: (0,) * nd, pipeline_mode=pl.Buffered(1))


def _row(v):
    return v.reshape(1, -1)


def _linear_kernel(x_ref, w_ref, o_ref):
    o_ref[...] = _mm(x_ref[...], w_ref[...]).astype(o_ref.dtype)


def _linear(x2d, w, out_dtype, tm, tn):
    m, k = x2d.shape
    n = w.shape[1]
    tm, tn = min(tm, m), min(tn, n)
    return pl.pallas_call(
        _linear_kernel,
        grid=(m // tm, n // tn),
        in_specs=[pl.BlockSpec((tm, k), lambda i, j: (i, 0)),
                  pl.BlockSpec((k, tn), lambda i, j: (0, j))],
        out_specs=pl.BlockSpec((tm, tn), lambda i, j: (i, j)),
        out_shape=jax.ShapeDtypeStruct((m, n), out_dtype),
        compiler_params=_cparams("parallel", "parallel"),
        name="linear",
    )(x2d, w)


def _xattn_kernel(x_ref, kv_ref, wq_ref, wo_ref, g_ref, b_ref, o_ref, oh_ref, *, heads, alpha):
    xb = x_ref[0]
    d = xb.shape[-1]
    e = d // heads
    q = (_mm(xb, wq_ref[...]) * (e ** -0.5)).astype(MXU_DTYPE)
    for h in range(heads):
        kh = kv_ref[0, :, h * e:(h + 1) * e]
        vh = kv_ref[0, :, d + h * e:d + (h + 1) * e]
        s = _mm_nt(q[:, h * e:(h + 1) * e], kh)
        m = jnp.max(s, axis=-1, keepdims=True)
        p = jnp.exp(s - m)
        den = jnp.sum(p, axis=-1, keepdims=True)
        oh_ref[:, h * e:(h + 1) * e] = _mm(p, vh) / den
    hout = _mm(oh_ref[...], wo_ref[...])
    o_ref[0] = _postnorm(alpha, xb, hout, g_ref[...], b_ref[...])


def _cross_attention(x, kv, wq, wo, g, b, alpha, tm=512):
    bsz, s, d = x.shape
    mlen = kv.shape[1]
    tm = min(tm, s)
    return pl.pallas_call(
        functools.partial(_xattn_kernel, heads=XATTN_HEADS, alpha=alpha),
        grid=(bsz, s // tm),
        in_specs=[pl.BlockSpec((1, tm, d), lambda bi, i: (bi, i, 0)),
                  pl.BlockSpec((1, mlen, 2 * d), lambda bi, i: (bi, 0, 0)),
                  _resident((d, d)), _resident((d, d)), _full((1, d)), _full((1, d))],
        out_specs=pl.BlockSpec((1, tm, d), lambda bi, i: (bi, i, 0)),
        out_shape=jax.ShapeDtypeStruct((bsz, s, d), F32),
        scratch_shapes=[pltpu.VMEM((tm, d), F32)],
        compiler_params=_cparams("parallel", "parallel"),
        name="cross_attention",
    )(x, kv, wq, wo, _row(g), _row(b))


def _mlp_kernel(x_ref, w1_ref, w2_ref, g_ref, b_ref, o_ref, *, fchunk, alpha):
    xb = x_ref[...]
    xm = xb.astype(MXU_DTYPE)
    ff = w1_ref.shape[1]
    acc = jnp.zeros(xb.shape, F32)
    for c in range(ff // fchunk):
        hid = jnp.dot(xm, w1_ref[:, c * fchunk:(c + 1) * fchunk], preferred_element_type=F32)
        hid = jnp.maximum(hid, 0.0)
        acc = acc + _mm(hid * hid, w2_ref[c * fchunk:(c + 1) * fchunk, :])
    o_ref[...] = _postnorm(alpha, xb, acc, g_ref[...], b_ref[...])


def _mlp(x2d, w1, w2, g, b, alpha, tm=512, fchunk=1024):
    n, d = x2d.shape
    ff = w1.shape[1]
    tm = min(tm, n)
    return pl.pallas_call(
        functools.partial(_mlp_kernel, fchunk=min(fchunk, ff), alpha=alpha),
        grid=(n // tm,),
        in_specs=[pl.BlockSpec((tm, d), lambda i: (i, 0)),
                  _resident((d, ff)), _resident((ff, d)), _full((1, d)), _full((1, d))],
        out_specs=pl.BlockSpec((tm, d), lambda i: (i, 0)),
        out_shape=jax.ShapeDtypeStruct((n, d), F32),
        compiler_params=_cparams("parallel"),
        name="sq_relu_mlp",
    )(x2d, w1, w2, _row(g), _row(b))


def _glu_kernel(x_ref, w_ref, b_ref, o_ref):
    h = _mm(x_ref[...], w_ref[...]) + b_ref[...]
    d = o_ref.shape[-1]
    o_ref[...] = h[:, :d] * _sigmoid(h[:, d:])


def _glu_proj(x2d, w, b, tm=512):
    n, d = x2d.shape
    tm = min(tm, n)
    return pl.pallas_call(
        _glu_kernel,
        grid=(n // tm,),
        in_specs=[pl.BlockSpec((tm, d), lambda i: (i, 0)), _resident((d, 2 * d)), _full((1, 2 * d))],
        out_specs=pl.BlockSpec((tm, d), lambda i: (i, 0)),
        out_shape=jax.ShapeDtypeStruct((n, d), F32),
        compiler_params=_cparams("parallel"),
        name="conv_glu_proj",
    )(x2d, w, _row(b))


CONV_HALO = 32


def _conv_kernel(u_ref, halo_ref, x_ref, dw_ref, dwb_ref, lng_ref, lnb_ref, wo_ref, bo_ref,
                 g_ref, b_ref, o_ref, cat_ref, sh_ref, *, alpha):
    i = pl.program_id(1)
    tm = u_ref.shape[1]
    cat_ref[0:CONV_HALO, :] = jnp.where(i == 0, 0.0, halo_ref[0])
    cat_ref[CONV_HALO:CONV_HALO + tm, :] = u_ref[0]
    acc = jnp.broadcast_to(dwb_ref[...], (tm, u_ref.shape[2]))
    off = CONV_HALO - (CONV_WIDTH - 1)
    for phase in range(SUBLANES):
        taps = [j for j in range(CONV_WIDTH) if (off + j) % SUBLANES == phase]
        span = max(off + j for j in taps) - phase + tm
        src = cat_ref
        if phase:
            sh_ref[0:span, :] = cat_ref[phase:phase + span, :]
            src = sh_ref
        for j in taps:
            a = off + j - phase
            acc = acc + dw_ref[j:j + 1, :] * src[a:a + tm, :]
    z = _layer_norm(acc, lng_ref[...], lnb_ref[...])
    z = z * _sigmoid(z)
    hout = _mm(z, wo_ref[...]) + bo_ref[...]
    o_ref[0] = _postnorm(alpha, x_ref[0], hout, g_ref[...], b_ref[...])


def _conv_mixer(x, w_in, b_in, dw, dw_b, ln_g, ln_b, w_out, b_out, g, b, alpha, tm=256):
    bsz, s, d = x.shape
    u = _glu_proj(x.reshape(bsz * s, d), w_in, b_in).reshape(bsz, s, d)
    tm = min(tm, s)
    hb = tm // CONV_HALO
    return pl.pallas_call(
        functools.partial(_conv_kernel, alpha=alpha),
        grid=(bsz, s // tm),
        in_specs=[pl.BlockSpec((1, tm, d), lambda bi, i: (bi, i, 0)),
                  pl.BlockSpec((1, CONV_HALO, d), lambda bi, i: (bi, jnp.maximum(i * hb - 1, 0), 0)),
                  pl.BlockSpec((1, tm, d), lambda bi, i: (bi, i, 0)),
                  _full((CONV_WIDTH, d)), _full((1, d)), _full((1, d)), _full((1, d)),
                  _resident((d, d)), _full((1, d)), _full((1, d)), _full((1, d))],
        out_specs=pl.BlockSpec((1, tm, d), lambda bi, i: (bi, i, 0)),
        out_shape=jax.ShapeDtypeStruct((bsz, s, d), F32),
        scratch_shapes=[pltpu.VMEM((CONV_HALO + tm, d), F32), pltpu.VMEM((CONV_HALO + tm, d), F32)],
        compiler_params=_cparams("parallel", "parallel"),
        name="conv_dw_ln_out",
    )(u, u, x, dw, _row(dw_b), _row(ln_g), _row(ln_b), w_out, _row(b_out), _row(g), _row(b))


def _t5_causal_bucket(dist):
    n = jnp.maximum(dist, 0)
    max_exact = REL_BUCKETS // 2
    nf = jnp.maximum(n, 1).astype(F32)
    large = max_exact + (jnp.log(nf / max_exact) / math.log(REL_MAX_DIST / max_exact)
                         * (REL_BUCKETS - max_exact)).astype(jnp.int32)
    large = jnp.minimum(large, REL_BUCKETS - 1)
    return jnp.where(n < max_exact, n, large)


def _qkv_kernel(x_ref, w_ref, o_ref, xs_ref, xp_ref, *, dil, qcols, qscale):
    tm = x_ref.shape[1]
    n = tm // dil
    if dil == 1:
        xp = x_ref[0]
    else:
        for j in range(xs_ref.shape[0]):
            cols = slice(j * LANES, (j + 1) * LANES)
            xs_ref[j] = x_ref[0, :, cols]
            for c in range(dil):
                xp_ref[c * n:(c + 1) * n, cols] = (
                    xs_ref[j, pl.ds(c, n, stride=dil), :].astype(xp_ref.dtype))
        xp = xp_ref[...]
    acc = _mm(xp, w_ref[...])
    for c in range(dil):
        o_ref[0, c, :, :qcols] = (acc[c * n:(c + 1) * n, :qcols] * qscale).astype(o_ref.dtype)
        o_ref[0, c, :, qcols:] = acc[c * n:(c + 1) * n, qcols:].astype(o_ref.dtype)


def _qkv_proj(x, w_g, dil, tm=512):
    bsz, s, d = x.shape
    sub = s // dil
    ncol = w_g.shape[1]
    tm = min(tm, s)
    return pl.pallas_call(
        functools.partial(_qkv_kernel, dil=dil, qcols=ncol // 3, qscale=DIL_HEAD_DIM ** -0.5),
        grid=(bsz, s // tm),
        in_specs=[pl.BlockSpec((1, tm, d), lambda bi, i: (bi, i, 0)),
                  _resident((d, ncol))],
        out_specs=pl.BlockSpec((1, dil, tm // dil, ncol), lambda bi, i: (bi, 0, i, 0)),
        out_shape=jax.ShapeDtypeStruct((bsz, dil, sub, ncol), MXU_DTYPE),
        scratch_shapes=[pltpu.VMEM((d // LANES, tm, LANES), F32), pltpu.VMEM((tm, d), MXU_DTYPE)],
        compiler_params=_cparams("parallel", "parallel"),
        name="dil_qkv_proj",
    )(x, w_g)


def _dil_bias_kernel(bucket_ref, relb_ref, bias_ref, *, heads, blk):
    bkt = bucket_ref[...]
    kcol = lax.broadcasted_iota(jnp.int32, bkt.shape, 1)
    for h in range(heads):
        acc = jnp.full(bkt.shape, NEG_BIG, F32)
        for n in range(REL_BUCKETS):
            acc = jnp.where(bkt == n, relb_ref[n, h], acc)
        bias_ref[h] = acc
        bias_ref[heads + h] = jnp.where(kcol < blk, NEG_BIG, acc)


def _dil_attn_kernel(bias_ref, q_ref, k_ref, v_ref, kp_ref, vp_ref, o_ref, l_ref,
                     kc_ref, vc_ref, *, heads, blk):
    i = pl.program_id(2)
    tq = q_ref.shape[2]
    kc_ref[0:blk, :] = kp_ref[0, 0]
    kc_ref[blk:blk + tq, :] = k_ref[0, 0]
    vc_ref[0:blk, :] = vp_ref[0, 0]
    vc_ref[blk:blk + tq, :] = v_ref[0, 0]
    lane = lax.broadcasted_iota(jnp.int32, (blk, LANES), 1)
    low = lane < DIL_HEAD_DIM

    def body(jb, carry):
        r0 = jb * blk
        table = jnp.where(i == 0, heads, 0) if jb == 0 else 0
        lse_all = jnp.zeros((blk, LANES), F32)
        for hp in range(heads // 2):
            cols = slice(hp * LANES, (hp + 1) * LANES)
            q2 = q_ref[0, 0, pl.ds(r0, blk), cols]
            k2 = kc_ref[pl.ds(r0, 2 * blk), cols]
            v2 = vc_ref[pl.ds(r0, 2 * blk), cols]
            outs = []
            for hh in range(2):
                h = 2 * hp + hh
                keep = low if hh == 0 else jnp.logical_not(low)
                qm = jnp.where(keep, q2, jnp.zeros_like(q2))
                s = _mm_nt(qm, k2) + bias_ref[table + h]
                m = jnp.max(s, axis=-1, keepdims=True)
                p = jnp.exp(s - m)
                den = jnp.sum(p, axis=-1, keepdims=True)
                outs.append(_mm(p, v2) / den)
                lse_all = jnp.where(lane == h, m + jnp.log(den), lse_all)
            o_ref[0, pl.ds(r0, blk), cols] = jnp.where(low, outs[0], outs[1]).astype(o_ref.dtype)
        l_ref[0, pl.ds(r0, blk), :] = lse_all
        return carry

    for jb in range(tq // blk):
        body(jb, 0)


def _dil_attention_group(qkv, rel_bias, window, dil, tq=512):
    bsz, _, sub, ncol = qkv.shape
    he = ncol // 3
    blk = window // dil
    tq = min(tq, sub)
    nsub = tq // blk
    qi = jnp.arange(blk)[:, None]
    kj = jnp.arange(2 * blk)[None, :]
    rel = qi + blk - kj
    bucket = jnp.where((rel >= 0) & (rel <= blk), _t5_causal_bucket(rel * dil), -1).astype(jnp.int32)

    def prev_map(col):
        return lambda bi, c, i: (bi, c, jnp.maximum(i * nsub - 1, 0), col)

    bias = pl.pallas_call(
        functools.partial(_dil_bias_kernel, heads=DIL_HEADS, blk=blk),
        in_specs=[pl.BlockSpec(memory_space=pltpu.VMEM), pl.BlockSpec(memory_space=pltpu.SMEM)],
        out_specs=pl.BlockSpec(memory_space=pltpu.VMEM),
        out_shape=jax.ShapeDtypeStruct((2 * DIL_HEADS, blk, 2 * blk), F32),
        name="dil_bias_table",
    )(bucket, rel_bias)

    o, lse = pl.pallas_call(
        functools.partial(_dil_attn_kernel, heads=DIL_HEADS, blk=blk),
        grid=(bsz, dil, sub // tq),
        in_specs=[_resident((2 * DIL_HEADS, blk, 2 * blk)),
                  pl.BlockSpec((1, 1, tq, he), lambda bi, c, i: (bi, c, i, 0)),
                  pl.BlockSpec((1, 1, tq, he), lambda bi, c, i: (bi, c, i, 1)),
                  pl.BlockSpec((1, 1, tq, he), lambda bi, c, i: (bi, c, i, 2)),
                  pl.BlockSpec((1, 1, blk, he), prev_map(1)),
                  pl.BlockSpec((1, 1, blk, he), prev_map(2))],
        out_specs=[pl.BlockSpec((1, tq, he), lambda bi, c, i: (bi, i, c)),
                   pl.BlockSpec((1, tq, LANES), lambda bi, c, i: (bi, i, c))],
        out_shape=[jax.ShapeDtypeStruct((bsz, sub, dil * he), MXU_DTYPE),
                   jax.ShapeDtypeStruct((bsz, sub, dil * LANES), F32)],
        scratch_shapes=[pltpu.VMEM((blk + tq, he), MXU_DTYPE),
                        pltpu.VMEM((blk + tq, he), MXU_DTYPE)],
        compiler_params=_cparams("parallel", "parallel", "parallel"),
        name="dil_attention",
    )(bias, qkv, qkv, qkv, qkv, qkv)
    s = sub * dil
    return o.reshape(bsz * s, he), lse.reshape(bsz * s, LANES)


def _dil_out_kernel(o1_ref, o2_ref, o3_ref, l1_ref, l2_ref, l3_ref, x_ref, e_ref, wo_ref, g_ref,
                    b_ref, out_ref, *, alpha):
    l1, l2, l3 = l1_ref[...], l2_ref[...], l3_ref[...]
    m = jnp.maximum(jnp.maximum(l1, l2), l3)
    e1, e2, e3 = jnp.exp(l1 - m), jnp.exp(l2 - m), jnp.exp(l3 - m)
    inv = 1.0 / (e1 + e2 + e3)
    spread = lambda w: _mm_exact_rhs(w, e_ref[...], passes=2)
    o = (spread(e1 * inv) * o1_ref[...] + spread(e2 * inv) * o2_ref[...]
         + spread(e3 * inv) * o3_ref[...])
    hout = _mm(o, wo_ref[...])
    out_ref[...] = _postnorm(alpha, x_ref[...], hout, g_ref[...], b_ref[...])


def _dil_mixer(x, w_qkv, w_out, rel_bias, g, b, alpha, tm=256):
    bsz, s, d = x.shape
    he = DIL_HEADS * DIL_HEAD_DIM
    outs, lses = [], []
    for gi, (window, dil) in enumerate(DIL_GROUPS):
        qkv = _qkv_proj(x, w_qkv[:, gi * 3 * he:(gi + 1) * 3 * he], dil)
        o, lse = _dil_attention_group(qkv, rel_bias, window, dil)
        outs.append(o)
        lses.append(lse)
    n = bsz * s
    tm = min(tm, n)
    row = pl.BlockSpec((tm, he), lambda i: (i, 0))
    lrow = pl.BlockSpec((tm, LANES), lambda i: (i, 0))
    expand = (jnp.arange(LANES)[:, None] == jnp.arange(he)[None, :] // DIL_HEAD_DIM).astype(BF16)
    out = pl.pallas_call(
        functools.partial(_dil_out_kernel, alpha=alpha),
        grid=(n // tm,),
        in_specs=[row] * 3 + [lrow] * 3 + [pl.BlockSpec((tm, d), lambda i: (i, 0)),
                                           _full((LANES, he)),
                                           _resident((he, d)), _full((1, d)), _full((1, d))],
        out_specs=pl.BlockSpec((tm, d), lambda i: (i, 0)),
        out_shape=jax.ShapeDtypeStruct((n, d), F32),
        compiler_params=_cparams("parallel"),
        name="dil_combine_out",
    )(*outs, *lses, x.reshape(n, d), expand, w_out, _row(g), _row(b))
    return out.reshape(bsz, s, d)


def _rwkv_prep_kernel(x_ref, xh_ref, mu_ref, wrkv_ref, w0_ref, w1_ref, w2_ref, a0_ref, a1_ref,
                      a2_ref, g1_ref, g2_ref, kk_ref, ka_ref,
                      r_o, k_o, v_o, lw_o, kk_o, b_o, g_o):
    i = pl.program_id(1)
    x = x_ref[0]
    tm = x.shape[0]
    prev_row = jnp.where(i == 0, 0.0, xh_ref[0, SUBLANES - 1:SUBLANES, :])
    rows = lax.broadcasted_iota(jnp.int32, (tm, 1), 0)
    xprev = jnp.where(rows == 0, prev_row, pltpu.roll(x, 1, 0))
    xx = xprev - x

    def mix(j):
        return x + xx * mu_ref[j:j + 1, :]

    r = _mm(mix(0), wrkv_ref[0])
    k = _mm(mix(1), wrkv_ref[1])
    v = _mm(mix(2), wrkv_ref[2])
    wl = w0_ref[...] + _mm(jnp.tanh(_mm(mix(3), w1_ref[...])), w2_ref[...])
    w_log = -_softplus(-wl) - 0.5
    a = _sigmoid(a0_ref[...] + _mm(_mm(mix(4), a1_ref[...]), a2_ref[...]))
    g = _mm(_sigmoid(_mm(mix(5), g1_ref[...])), g2_ref[...])
    kk = k * kk_ref[...]
    r_o[0] = r
    k_o[0] = k * (1.0 + (a - 1.0) * ka_ref[...])
    v_o[0] = v
    lw_o[0] = -jnp.exp(w_log)
    kk_o[0] = kk
    b_o[0] = kk * a
    g_o[0] = g


def _tri_inv(mats, blk):
    n = mats[0].shape[0]
    r = lax.broadcasted_iota(jnp.int32, (n, n), 0)
    c = lax.broadcasted_iota(jnp.int32, (n, n), 1)

    def same(bs):
        sh = int(math.log2(bs))
        return lax.shift_right_logical(r, sh) == lax.shift_right_logical(c, sh)

    base = same(TRI_BASE)
    eye = jnp.where(r == c, 1.0, 0.0)
    ps = [jnp.where(base, a, 0.0) for a in mats]
    ts = [eye + p for p in ps]
    ps = [_mm(p, p) for p in ps]
    m = 2
    while 2 * m < TRI_BASE:
        both = [_mm(p, jnp.concatenate([p, t], axis=1)) for p, t in zip(ps, ts)]
        ts = [t + x[:, n:] for t, x in zip(ts, both)]
        ps = [x[:, :n] for x in both]
        m *= 2
    ts = [t + _mm(p, t) for p, t in zip(ps, ts)]
    bs = TRI_BASE
    while bs < blk:
        sel = same(2 * bs) & jnp.logical_not(same(bs))
        nblk = n // bs
        rows = lambda z, i: z[i * bs:(i + 1) * bs]
        t_odd = [jnp.concatenate([rows(t, i) for i in range(1, nblk, 2)], axis=0) for t in ts]
        lo = [_mm(to, jnp.where(sel, a, 0.0)) for to, a in zip(t_odd, mats)]
        upd = [_mm(l, t) for l, t in zip(lo, ts)]
        ts = [jnp.concatenate([rows(t, i) + rows(u, i // 2) if i % 2 else rows(t, i)
                               for i in range(nblk)], axis=0)
              for t, u in zip(ts, upd)]
        bs *= 2
    return ts


def _rwkv_scan_kernel(r_ref, k_ref, v_ref, lw_ref, kk_ref, b_ref, lg_ref, lb_ref, rk_ref,
                      y_ref, st_ref, *, chunk):
    @pl.when(pl.program_id(2) == 0)
    def _():
        st_ref[...] = jnp.zeros_like(st_ref)

    L = chunk
    n2 = 2 * L
    tb, width = r_ref.shape[1], r_ref.shape[2]
    chunks = range(tb // L)
    pairs = range(width // LANES)
    probs = [(p, ch) for ch in chunks for p in pairs]
    sh = int(math.log2(L))
    lane = lax.broadcasted_iota(jnp.int32, (L, LANES), 1)
    head0 = lane < RWKV_HEAD
    rr = lax.broadcasted_iota(jnp.int32, (n2, n2), 0)
    cc = lax.broadcasted_iota(jnp.int32, (n2, n2), 1)
    rt, ct = rr & (L - 1), cc & (L - 1)
    strict, incl, eye = rt > ct, rt >= ct, rr == cc
    tr = lax.broadcasted_iota(jnp.int32, (tb, tb), 0)
    tc = lax.broadcasted_iota(jnp.int32, (tb, tb), 1)
    same_chunk = lax.shift_right_logical(tr, sh) == lax.shift_right_logical(tc, sh)
    head0_tb = lax.broadcasted_iota(jnp.int32, (tb, LANES), 1) < RWKV_HEAD

    def head_sum(z):
        out = []
        for p in pairs:
            zp = z[:, p * LANES:(p + 1) * LANES]
            s0 = jnp.sum(jnp.where(head0_tb, zp, 0.0), axis=-1, keepdims=True)
            s1 = jnp.sum(jnp.where(head0_tb, 0.0, zp), axis=-1, keepdims=True)
            out.append(jnp.where(head0_tb, s0, s1))
        return jnp.concatenate(out, axis=1)

    r, k, v = r_ref[0], k_ref[0], v_ref[0]
    lw, kk = lw_ref[0], kk_ref[0]
    inv_norm = 1.0 / jnp.maximum(jnp.sqrt(head_sum(kk * kk)), 1e-12)
    kn = kk * inv_norm
    b = b_ref[0] * inv_norm
    tri = jnp.where(same_chunk & (tr >= tc), 1.0, 0.0).astype(BF16)
    c = _mm_exact_lhs(tri, lw)
    c_end = jnp.concatenate(
        [jnp.broadcast_to(c[(ch + 1) * L - 1:(ch + 1) * L, :], (L, width)) for ch in chunks], axis=0)
    e_inv = jnp.exp(-c)
    e_end = jnp.exp(c_end - c)
    p_end = jnp.exp(c_end)
    a_t = -kn * jnp.exp(c - lw)
    r_t = r * jnp.exp(c)
    bh, kh = b * e_inv, k * e_inv
    bb, kb = b * e_end, k * e_end

    def stack(z, prob):
        p, ch = prob
        z = z[ch * L:(ch + 1) * L, p * LANES:(p + 1) * LANES]
        return jnp.concatenate([jnp.where(head0, z, 0.0), jnp.where(head0, 0.0, z)], axis=0)

    a_s = [stack(a_t, q) for q in probs]
    r_s = [stack(r_t, q) for q in probs]
    v_s = [stack(v, q) for q in probs]
    ms = [_mm_nt(jnp.concatenate([a, rs], axis=0),
                 jnp.concatenate([stack(bh, q), stack(kh, q)], axis=0))
          for a, rs, q in zip(a_s, r_s, probs)]
    ts = _tri_inv([jnp.where(strict, m[:n2, :n2], 0.0) for m in ms], L)
    av = [_mm(jnp.where(strict, m[:n2, n2:], 0.0), vs) for m, vs in zip(ms, v_s)]
    tu = [_mm(t, jnp.concatenate([a, x], axis=1)) for t, a, x in zip(ts, a_s, av)]
    zeros = jnp.zeros((n2, LANES), F32)
    incl2 = jnp.concatenate([incl, incl], axis=1)
    big = [_mm(jnp.concatenate([jnp.where(incl2, m[n2:], 0.0),
                                jnp.concatenate([stack(bb, q), stack(kb, q)], axis=0).T], axis=0),
               jnp.concatenate([x, jnp.concatenate([zeros, vs], axis=1)], axis=0))
           for m, q, x, vs in zip(ms, probs, tu, v_s)]
    rg_lhs, y0, h0 = {}, {}, {}
    for i, (p, ch) in enumerate(probs):
        p_last = p_end[ch * L:ch * L + 1, p * LANES:(p + 1) * LANES]
        rg_lhs[p, ch] = big[i][:, :LANES] + jnp.concatenate(
            [r_s[i], jnp.where(eye, p_last, 0.0)], axis=0)
        y0[p, ch] = big[i][:n2, LANES:]
        h0[p, ch] = big[i][n2:, LANES:]

    st = [st_ref[p] for p in pairs]
    ys = [[] for _ in pairs]
    for ch in chunks:
        for p in pairs:
            rg = _mm(rg_lhs[p, ch], st[p])
            y2 = rg[:n2] + y0[p, ch]
            st[p] = rg[n2:] + h0[p, ch]
            ys[p].append(y2[:L] + y2[L:])
    for p in pairs:
        st_ref[p] = st[p]

    y = jnp.concatenate([jnp.concatenate(yp, axis=0) for yp in ys], axis=1)
    inv_n = 1.0 / RWKV_HEAD
    yc = y - head_sum(y) * inv_n
    var = head_sum(yc * yc) * inv_n
    yn = yc * lax.rsqrt(var + RWKV_GN_EPS) * lg_ref[...] + lb_ref[...]
    y_ref[0] = yn + head_sum(r * k * rk_ref[...]) * v


def _rwkv_post_kernel(y_ref, gate_ref, x_ref, wo_ref, g_ref, b_ref, o_ref, *, alpha):
    hout = _mm(y_ref[...] * gate_ref[...], wo_ref[...])
    o_ref[...] = _postnorm(alpha, x_ref[...], hout, g_ref[...], b_ref[...])


def _rwkv_mixer(x, mu, w_rkv, w0, w1, w2, a0, a1, a2, g1, g2, k_k, k_a, r_k, lnx_g, lnx_b,
                w_out, g, b, alpha, tm=256, tblk=256):
    bsz, s, d = x.shape
    tm = min(tm, s)
    tile = pl.BlockSpec((1, tm, d), lambda bi, i: (bi, i, 0))
    hb = tm // SUBLANES
    lora = w1.shape[1]
    glora = g1.shape[1]
    outs = pl.pallas_call(
        _rwkv_prep_kernel,
        grid=(bsz, s // tm),
        in_specs=[tile,
                  pl.BlockSpec((1, SUBLANES, d), lambda bi, i: (bi, jnp.maximum(i * hb - 1, 0), 0)),
                  _full((6, d)), _resident((3, d, d)), _full((1, d)),
                  _full((d, lora)), _full((lora, d)), _full((1, d)),
                  _full((d, a1.shape[1])), _full((a2.shape[0], d)),
                  _full((d, glora)), _full((glora, d)),
                  _full((1, d)), _full((1, d))],
        out_specs=[tile] * 7,
        out_shape=[jax.ShapeDtypeStruct((bsz, s, d), F32)] * 7,
        compiler_params=_cparams("parallel", "parallel"),
        name="rwkv_prep",
    )(x, x, mu, w_rkv, _row(w0), w1, w2, _row(a0), a1, a2, g1, g2, _row(k_k), _row(k_a))
    r, k, v, lw, kk, bvec, gate = outs

    tblk = min(tblk, s)
    width = SCAN_PAIRS * LANES
    blk = pl.BlockSpec((1, tblk, width), lambda bi, hp, i: (bi, i, hp))
    chan = pl.BlockSpec((1, width), lambda bi, hp, i: (0, hp))
    y = pl.pallas_call(
        functools.partial(_rwkv_scan_kernel, chunk=RWKV_CHUNK),
        grid=(bsz, d // width, s // tblk),
        in_specs=[blk] * 6 + [chan] * 3,
        out_specs=blk,
        out_shape=jax.ShapeDtypeStruct((bsz, s, d), F32),
        scratch_shapes=[pltpu.VMEM((SCAN_PAIRS, LANES, LANES), F32)],
        compiler_params=_cparams("parallel", "parallel", "arbitrary"),
        name="rwkv_scan",
    )(r, k, v, lw, kk, bvec, _row(lnx_g), _row(lnx_b), _row(r_k))

    n = bsz * s
    tm2 = min(tm, n)
    row = pl.BlockSpec((tm2, d), lambda i: (i, 0))
    flat = lambda z: z.reshape(n, d)
    out = pl.pallas_call(
        functools.partial(_rwkv_post_kernel, alpha=alpha),
        grid=(n // tm2,),
        in_specs=[row] * 3 + [_resident((d, d)), _full((1, d)), _full((1, d))],
        out_specs=row,
        out_shape=jax.ShapeDtypeStruct((n, d), F32),
        compiler_params=_cparams("parallel"),
        name="rwkv_post",
    )(flat(y), flat(gate), flat(x), w_out, _row(g), _row(b))
    return out.reshape(bsz, s, d)


def kernel(x, mem, rel_bias, a_w_in, a_b_in, a_dw, a_dw_b, a_ln_g, a_ln_b, a_w_out, a_b_out, b_w_qkv, b_w_out, c_mu, c_w_rkv, c_w0, c_w1, c_w2, c_a0, c_a1, c_a2, c_g1, c_g2, c_k_k, c_k_a, c_r_k, c_lnx_g, c_lnx_b, c_w_out, x_w_q, x_w_kv, x_w_out, m_w1, m_w2, ln_g, ln_b):
    depth = ln_g.shape[0]
    alpha = (2 * depth) ** 0.25
    bsz, s, d = x.shape
    mlen = mem.shape[1]
    w = lambda t: t.astype(MXU_DTYPE)
    mem2d = mem.reshape(bsz * mlen, d)
    for i in range(depth):
        kind, j = i % 3, i // 3
        if kind == 0:
            x = _conv_mixer(x, w(a_w_in[j]), a_b_in[j], a_dw[j], a_dw_b[j], a_ln_g[j], a_ln_b[j],
                            w(a_w_out[j]), a_b_out[j], ln_g[i, 0], ln_b[i, 0], alpha)
        elif kind == 1:
            x = _dil_mixer(x, w(b_w_qkv[j]), w(b_w_out[j]), rel_bias, ln_g[i, 0], ln_b[i, 0], alpha)
        else:
            x = _rwkv_mixer(x, c_mu[j], w(c_w_rkv[j]), c_w0[j], w(c_w1[j]), w(c_w2[j]), c_a0[j],
                            w(c_a1[j]), w(c_a2[j]), w(c_g1[j]), w(c_g2[j]), c_k_k[j], c_k_a[j],
                            c_r_k[j], c_lnx_g[j], c_lnx_b[j], w(c_w_out[j]),
                            ln_g[i, 0], ln_b[i, 0], alpha)
        kv = _linear(mem2d, w(x_w_kv[i]), MXU_DTYPE, 512, 1024).reshape(bsz, mlen, 2 * d)
        x = _cross_attention(x, kv, w(x_w_q[i]), w(x_w_out[i]), ln_g[i, 1], ln_b[i, 1], alpha)
        x = _mlp(x.reshape(bsz * s, d), w(m_w1[i]), w(m_w2[i]), ln_g[i, 2], ln_b[i, 2],
                 alpha).reshape(bsz, s, d)
    return x
```

```python
import functools
import math

import jax
import jax.numpy as jnp
from jax import lax
from jax.experimental import pallas as pl
from jax.experimental.pallas import tpu as pltpu

F32 = jnp.float32
BF16 = jnp.bfloat16
MXU_DTYPE = jnp.bfloat16

CONV_WIDTH = 31
DIL_GROUPS = ((128, 1), (512, 4), (2048, 16))
DIL_HEADS = 16
DIL_HEAD_DIM = 64
REL_BUCKETS = 32
REL_MAX_DIST = 2048
RWKV_HEAD = 64
RWKV_GN_EPS = 64e-5
XATTN_HEADS = 4
LN_EPS = 1e-5
NEG_BIG = -1e30

LANES = 128
SUBLANES = 8
VMEM_LIMIT_BYTES = 56 * 1024 * 1024

RWKV_CHUNK = 64
TRI_BASE = 8
SCAN_PAIRS = 4


def _cparams(*sem):
    return pltpu.CompilerParams(dimension_semantics=sem, vmem_limit_bytes=VMEM_LIMIT_BYTES)


def _mm(a, b):
    return jnp.dot(a.astype(MXU_DTYPE), b.astype(MXU_DTYPE), preferred_element_type=F32)


def _mm_nt(a, b):
    return lax.dot_general(a.astype(MXU_DTYPE), b.astype(MXU_DTYPE),
                           (((1,), (1,)), ((), ())), preferred_element_type=F32)


def _mm_tn(a, b):
    return _mm(a.T, b)


def _split3(x):
    hi = x.astype(BF16)
    r1 = x - hi.astype(F32)
    mid = r1.astype(BF16)
    lo = (r1 - mid.astype(F32)).astype(BF16)
    return hi, mid, lo


def _mm_exact_rhs(x, e, passes=3):
    pieces = _split3(x)[:passes]
    return sum(jnp.dot(p, e, preferred_element_type=F32) for p in pieces)


def _mm_exact_lhs(e, x):
    hi, mid, lo = _split3(x)
    dot = functools.partial(jnp.dot, preferred_element_type=F32)
    return dot(e, hi) + dot(e, mid) + dot(e, lo)


def _sigmoid(z):
    return 1.0 / (1.0 + jnp.exp(-z))


def _softplus(z):
    return jnp.maximum(z, 0.0) + jnp.log(1.0 + jnp.exp(-jnp.abs(z)))


def _layer_norm(z, g, b, eps=LN_EPS):
    mu = jnp.mean(z, axis=-1, keepdims=True)
    zc = z - mu
    var = jnp.mean(zc * zc, axis=-1, keepdims=True)
    return zc * lax.rsqrt(var + eps) * g + b


def _postnorm(alpha, x, h, g, b):
    return _layer_norm(alpha * x + h, g, b)


def _full(shape):
    nd = len(shape)
    return pl.BlockSpec(shape, lambda *_: (0,) * nd)


def _resident(shape):
    nd = len(shape)
    return pl.BlockSpec(shape, lambda *_: (0,) * nd, pipeline_mode=pl.Buffered(1))


def _row(v):
    return v.reshape(1, -1)


def _linear_kernel(x_ref, w_ref, o_ref):
    o_ref[...] = _mm(x_ref[...], w_ref[...]).astype(o_ref.dtype)


def _linear(x2d, w, out_dtype, tm, tn):
    m, k = x2d.shape
    n = w.shape[1]
    tm, tn = min(tm, m), min(tn, n)
    return pl.pallas_call(
        _linear_kernel,
        grid=(m // tm, n // tn),
        in_specs=[pl.BlockSpec((tm, k), lambda i, j: (i, 0)),
                  pl.BlockSpec((k, tn), lambda i, j: (0, j))],
        out_specs=pl.BlockSpec((tm, tn), lambda i, j: (i, j)),
        out_shape=jax.ShapeDtypeStruct((m, n), out_dtype),
        compiler_params=_cparams("parallel", "parallel"),
        name="linear",
    )(x2d, w)


def _xattn_kernel(x_ref, kv_ref, wq_ref, wo_ref, g_ref, b_ref, o_ref, oh_ref, *, heads, alpha):
    xb = x_ref[0]
    d = xb.shape[-1]
    e = d // heads
    q = (_mm(xb, wq_ref[...]) * (e ** -0.5)).astype(MXU_DTYPE)
    for h in range(heads):
        kh = kv_ref[0, :, h * e:(h + 1) * e]
        vh = kv_ref[0, :, d + h * e:d + (h + 1) * e]
        s = _mm_nt(q[:, h * e:(h + 1) * e], kh)
        m = jnp.max(s, axis=-1, keepdims=True)
        p = jnp.exp(s - m)
        den = jnp.sum(p, axis=-1, keepdims=True)
        oh_ref[:, h * e:(h + 1) * e] = _mm(p, vh) / den
    hout = _mm(oh_ref[...], wo_ref[...])
    o_ref[0] = _postnorm(alpha, xb, hout, g_ref[...], b_ref[...])


def _cross_attention(x, kv, wq, wo, g, b, alpha, tm=512):
    bsz, s, d = x.shape
    mlen = kv.shape[1]
    tm = min(tm, s)
    return pl.pallas_call(
        functools.partial(_xattn_kernel, heads=XATTN_HEADS, alpha=alpha),
        grid=(bsz, s // tm),
        in_specs=[pl.BlockSpec((1, tm, d), lambda bi, i: (bi, i, 0)),
                  pl.BlockSpec((1, mlen, 2 * d), lambda bi, i: (bi, 0, 0)),
                  _resident((d, d)), _resident((d, d)), _full((1, d)), _full((1, d))],
        out_specs=pl.BlockSpec((1, tm, d), lambda bi, i: (bi, i, 0)),
        out_shape=jax.ShapeDtypeStruct((bsz, s, d), F32),
        scratch_shapes=[pltpu.VMEM((tm, d), F32)],
        compiler_params=_cparams("parallel", "parallel"),
        name="cross_attention",
    )(x, kv, wq, wo, _row(g), _row(b))


def _mlp_kernel(x_ref, w1_ref, w2_ref, g_ref, b_ref, o_ref, *, fchunk, alpha):
    xb = x_ref[...]
    xm = xb.astype(MXU_DTYPE)
    ff = w1_ref.shape[1]
    acc = jnp.zeros(xb.shape, F32)
    for c in range(ff // fchunk):
        hid = jnp.dot(xm, w1_ref[:, c * fchunk:(c + 1) * fchunk], preferred_element_type=F32)
        hid = jnp.maximum(hid, 0.0)
        acc = acc + _mm(hid * hid, w2_ref[c * fchunk:(c + 1) * fchunk, :])
    o_ref[...] = _postnorm(alpha, xb, acc, g_ref[...], b_ref[...])


def _mlp(x2d, w1, w2, g, b, alpha, tm=512, fchunk=1024):
    n, d = x2d.shape
    ff = w1.shape[1]
    tm = min(tm, n)
    return pl.pallas_call(
        functools.partial(_mlp_kernel, fchunk=min(fchunk, ff), alpha=alpha),
        grid=(n // tm,),
        in_specs=[pl.BlockSpec((tm, d), lambda i: (i, 0)),
                  _resident((d, ff)), _resident((ff, d)), _full((1, d)), _full((1, d))],
        out_specs=pl.BlockSpec((tm, d), lambda i: (i, 0)),
        out_shape=jax.ShapeDtypeStruct((n, d), F32),
        compiler_params=_cparams("parallel"),
        name="sq_relu_mlp",
    )(x2d, w1, w2, _row(g), _row(b))


def _glu_kernel(x_ref, w_ref, b_ref, o_ref):
    h = _mm(x_ref[...], w_ref[...]) + b_ref[...]
    d = o_ref.shape[-1]
    o_ref[...] = h[:, :d] * _sigmoid(h[:, d:])


def _glu_proj(x2d, w, b, tm=512):
    n, d = x2d.shape
    tm = min(tm, n)
    return pl.pallas_call(
        _glu_kernel,
        grid=(n // tm,),
        in_specs=[pl.BlockSpec((tm, d), lambda i: (i, 0)), _resident((d, 2 * d)), _full((1, 2 * d))],
        out_specs=pl.BlockSpec((tm, d), lambda i: (i, 0)),
        out_shape=jax.ShapeDtypeStruct((n, d), F32),
        compiler_params=_cparams("parallel"),
        name="conv_glu_proj",
    )(x2d, w, _row(b))


CONV_HALO = 32


def _conv_kernel(u_ref, halo_ref, x_ref, dw_ref, dwb_ref, lng_ref, lnb_ref, wo_ref, bo_ref,
                 g_ref, b_ref, o_ref, cat_ref, sh_ref, *, alpha):
    i = pl.program_id(1)
    tm = u_ref.shape[1]
    cat_ref[0:CONV_HALO, :] = jnp.where(i == 0, 0.0, halo_ref[0])
    cat_ref[CONV_HALO:CONV_HALO + tm, :] = u_ref[0]
    acc = jnp.broadcast_to(dwb_ref[...], (tm, u_ref.shape[2]))
    off = CONV_HALO - (CONV_WIDTH - 1)
    for phase in range(SUBLANES):
        taps = [j for j in range(CONV_WIDTH) if (off + j) % SUBLANES == phase]
        span = max(off + j for j in taps) - phase + tm
        src = cat_ref
        if phase:
            sh_ref[0:span, :] = cat_ref[phase:phase + span, :]
            src = sh_ref
        for j in taps:
            a = off + j - phase
            acc = acc + dw_ref[j:j + 1, :] * src[a:a + tm, :]
    z = _layer_norm(acc, lng_ref[...], lnb_ref[...])
    z = z * _sigmoid(z)
    hout = _mm(z, wo_ref[...]) + bo_ref[...]
    o_ref[0] = _postnorm(alpha, x_ref[0], hout, g_ref[...], b_ref[...])


def _conv_mixer(x, w_in, b_in, dw, dw_b, ln_g, ln_b, w_out, b_out, g, b, alpha, tm=256):
    bsz, s, d = x.shape
    u = _glu_proj(x.reshape(bsz * s, d), w_in, b_in).reshape(bsz, s, d)
    tm = min(tm, s)
    hb = tm // CONV_HALO
    return pl.pallas_call(
        functools.partial(_conv_kernel, alpha=alpha),
        grid=(bsz, s // tm),
        in_specs=[pl.BlockSpec((1, tm, d), lambda bi, i: (bi, i, 0)),
                  pl.BlockSpec((1, CONV_HALO, d), lambda bi, i: (bi, jnp.maximum(i * hb - 1, 0), 0)),
                  pl.BlockSpec((1, tm, d), lambda bi, i: (bi, i, 0)),
                  _full((CONV_WIDTH, d)), _full((1, d)), _full((1, d)), _full((1, d)),
                  _resident((d, d)), _full((1, d)), _full((1, d)), _full((1, d))],
        out_specs=pl.BlockSpec((1, tm, d), lambda bi, i: (bi, i, 0)),
        out_shape=jax.ShapeDtypeStruct((bsz, s, d), F32),
        scratch_shapes=[pltpu.VMEM((CONV_HALO + tm, d), F32), pltpu.VMEM((CONV_HALO + tm, d), F32)],
        compiler_params=_cparams("parallel", "parallel"),
        name="conv_dw_ln_out",
    )(u, u, x, dw, _row(dw_b), _row(ln_g), _row(ln_b), w_out, _row(b_out), _row(g), _row(b))


def _t5_causal_bucket(dist):
    n = jnp.maximum(dist, 0)
    max_exact = REL_BUCKETS // 2
    nf = jnp.maximum(n, 1).astype(F32)
    large = max_exact + (jnp.log(nf / max_exact) / math.log(REL_MAX_DIST / max_exact)
                         * (REL_BUCKETS - max_exact)).astype(jnp.int32)
    large = jnp.minimum(large, REL_BUCKETS - 1)
    return jnp.where(n < max_exact, n, large)


def _qkv_kernel(x_ref, w_ref, o_ref, xs_ref, xp_ref, *, dil, qscale):
    tm = x_ref.shape[1]
    n = tm // dil
    if dil == 1:
        xp = x_ref[0]
    else:
        for j in range(xs_ref.shape[0]):
            cols = slice(j * LANES, (j + 1) * LANES)
            xs_ref[j] = x_ref[0, :, cols]
            for c in range(dil):
                xp_ref[c * n:(c + 1) * n, cols] = (
                    xs_ref[j, pl.ds(c, n, stride=dil), :].astype(xp_ref.dtype))
        xp = xp_ref[...]
    acc = _mm(xp, w_ref[...])
    nblk = o_ref.shape[2]
    for c in range(dil):
        for j in range(nblk):
            tile = acc[c * n:(c + 1) * n, j * LANES:(j + 1) * LANES]
            if 3 * j < nblk:
                tile = tile * qscale
            o_ref[0, c, j] = tile.astype(o_ref.dtype)


def _qkv_proj(x, w_qkv, group, dil, tm=512):
    bsz, s, d = x.shape
    sub = s // dil
    ncol = w_qkv.shape[1] // len(DIL_GROUPS)
    nblk = ncol // LANES
    tm = min(tm, s)
    return pl.pallas_call(
        functools.partial(_qkv_kernel, dil=dil, qscale=DIL_HEAD_DIM ** -0.5),
        grid=(bsz, s // tm),
        in_specs=[pl.BlockSpec((1, tm, d), lambda bi, i: (bi, i, 0)),
                  pl.BlockSpec((d, ncol), lambda bi, i: (0, group), pipeline_mode=pl.Buffered(1))],
        out_specs=pl.BlockSpec((1, dil, nblk, tm // dil, LANES), lambda bi, i: (bi, 0, 0, i, 0)),
        out_shape=jax.ShapeDtypeStruct((bsz, dil, nblk, sub, LANES), MXU_DTYPE),
        scratch_shapes=[pltpu.VMEM((d // LANES, tm, LANES), F32), pltpu.VMEM((tm, d), MXU_DTYPE)],
        compiler_params=_cparams("parallel", "parallel"),
        name="dil_qkv_proj",
    )(x, w_qkv)


def _dil_bias_kernel(bucket_ref, relb_ref, bias_ref, *, heads, blk):
    bkt = bucket_ref[...]
    kcol = lax.broadcasted_iota(jnp.int32, bkt.shape, 1)
    for h in range(heads):
        acc = jnp.full(bkt.shape, NEG_BIG, F32)
        for n in range(REL_BUCKETS):
            acc = jnp.where(bkt == n, relb_ref[n, h], acc)
        bias_ref[h] = acc
        bias_ref[heads + h] = jnp.where(kcol < blk, NEG_BIG, acc)


def _dil_attn_kernel(bias_ref, q_ref, k_ref, v_ref, kp_ref, vp_ref, o_ref, l_ref,
                     kc_ref, vc_ref, *, heads, blk):
    i = pl.program_id(2)
    tq = q_ref.shape[3]
    kc_ref[:, 0:blk, :] = kp_ref[0, 0]
    kc_ref[:, blk:blk + tq, :] = k_ref[0, 0]
    vc_ref[:, 0:blk, :] = vp_ref[0, 0]
    vc_ref[:, blk:blk + tq, :] = v_ref[0, 0]
    lane = lax.broadcasted_iota(jnp.int32, (blk, LANES), 1)
    low = lane < DIL_HEAD_DIM

    for jb in range(tq // blk):
        rows = slice(jb * blk, (jb + 1) * blk)
        keys = slice(jb * blk, (jb + 2) * blk)
        table = jnp.where(i == 0, heads, 0) if jb == 0 else 0

        def pair(hp, lse_all):
            q2 = q_ref[0, 0, hp, rows, :]
            k2 = kc_ref[hp, keys, :]
            v2 = vc_ref[hp, keys, :]
            outs = []
            for hh in range(2):
                h = 2 * hp + hh
                keep = low if hh == 0 else jnp.logical_not(low)
                qm = jnp.where(keep, q2, jnp.zeros_like(q2))
                s = _mm_nt(qm, k2) + bias_ref[table + h]
                m = jnp.max(s, axis=-1, keepdims=True)
                p = jnp.exp(s - m)
                den = jnp.sum(p, axis=-1, keepdims=True)
                outs.append(_mm(p, v2) / den)
                lse_all = jnp.where(lane == h, m + jnp.log(den), lse_all)
            o_ref[0, 0, hp, rows, :] = jnp.where(low, outs[0], outs[1]).astype(o_ref.dtype)
            return lse_all

        lse_all = jnp.zeros((blk, LANES), F32)
        for hp in range(heads // 2):
            lse_all = pair(hp, lse_all)
        l_ref[0, 0, rows, :] = lse_all


def _dil_attention_group(qkv, rel_bias, window, dil, tq=512):
    bsz, _, nblk, sub, _ = qkv.shape
    hb = nblk // 3
    blk = window // dil
    tq = min(tq, sub)
    nsub = tq // blk
    qi = jnp.arange(blk)[:, None]
    kj = jnp.arange(2 * blk)[None, :]
    rel = qi + blk - kj
    bucket = jnp.where((rel >= 0) & (rel <= blk), _t5_causal_bucket(rel * dil), -1).astype(jnp.int32)

    bias = pl.pallas_call(
        functools.partial(_dil_bias_kernel, heads=DIL_HEADS, blk=blk),
        in_specs=[pl.BlockSpec(memory_space=pltpu.VMEM), pl.BlockSpec(memory_space=pltpu.SMEM)],
        out_specs=pl.BlockSpec(memory_space=pltpu.VMEM),
        out_shape=jax.ShapeDtypeStruct((2 * DIL_HEADS, blk, 2 * blk), F32),
        name="dil_bias_table",
    )(bucket, rel_bias)

    def cur(which):
        return pl.BlockSpec((1, 1, hb, tq, LANES), lambda bi, c, i: (bi, c, which, i, 0))

    def prev(which):
        return pl.BlockSpec((1, 1, hb, blk, LANES),
                            lambda bi, c, i: (bi, c, which, jnp.maximum(i * nsub - 1, 0), 0))

    return pl.pallas_call(
        functools.partial(_dil_attn_kernel, heads=DIL_HEADS, blk=blk),
        grid=(bsz, dil, sub // tq),
        in_specs=[_resident((2 * DIL_HEADS, blk, 2 * blk)), cur(0), cur(1), cur(2), prev(1), prev(2)],
        out_specs=[pl.BlockSpec((1, 1, hb, tq, LANES), lambda bi, c, i: (bi, c, 0, i, 0)),
                   pl.BlockSpec((1, 1, tq, LANES), lambda bi, c, i: (bi, c, i, 0))],
        out_shape=[jax.ShapeDtypeStruct((bsz, dil, hb, sub, LANES), MXU_DTYPE),
                   jax.ShapeDtypeStruct((bsz, dil, sub, LANES), F32)],
        scratch_shapes=[pltpu.VMEM((hb, blk + tq, LANES), MXU_DTYPE),
                        pltpu.VMEM((hb, blk + tq, LANES), MXU_DTYPE)],
        compiler_params=_cparams("parallel", "parallel", "parallel"),
        name="dil_attention",
    )(bias, qkv, qkv, qkv, qkv, qkv)


def _dil_out_kernel(o1_ref, o2_ref, o3_ref, l1_ref, l2_ref, l3_ref, x_ref, e_ref, wo_ref, g_ref,
                    b_ref, out_ref, os_ref, ls_ref, *, alpha, dils):
    tm = x_ref.shape[1]
    o_refs, l_refs = (o1_ref, o2_ref, o3_ref), (l1_ref, l2_ref, l3_ref)

    def natural(dst_ref, src, dil):
        if dil == 1:
            return src(0).astype(F32)
        n = tm // dil
        for c in range(dil):
            dst_ref[pl.ds(c, n, stride=dil), :] = src(c).astype(F32)
        return dst_ref[...]

    l1, l2, l3 = [natural(ls_ref.at[g], lambda c, r=l_refs[g]: r[0, c], dils[g]) for g in range(3)]
    m = jnp.maximum(jnp.maximum(l1, l2), l3)
    es = [jnp.exp(l1 - m), jnp.exp(l2 - m), jnp.exp(l3 - m)]
    inv = 1.0 / (es[0] + es[1] + es[2])
    ws = [_mm_exact_rhs(e * inv, e_ref[...], passes=2) for e in es]
    cols = []
    for j in range(o1_ref.shape[2]):
        blk = slice(j * LANES, (j + 1) * LANES)
        cols.append(sum(ws[g][:, blk] * natural(os_ref.at[g, j],
                                                lambda c, r=o_refs[g], j=j: r[0, c, j], dils[g])
                        for g in range(3)))
    hout = _mm(jnp.concatenate(cols, axis=1), wo_ref[...])
    out_ref[0] = _postnorm(alpha, x_ref[0], hout, g_ref[...], b_ref[...])


def _dil_mixer(x, w_qkv, w_out, rel_bias, g, b, alpha, tm=256):
    bsz, s, d = x.shape
    he = DIL_HEADS * DIL_HEAD_DIM
    hb = he // LANES
    dils = tuple(dil for _, dil in DIL_GROUPS)
    outs, lses = [], []
    for gi, (window, dil) in enumerate(DIL_GROUPS):
        qkv = _qkv_proj(x, w_qkv, gi, dil)
        o, lse = _dil_attention_group(qkv, rel_bias, window, dil)
        outs.append(o)
        lses.append(lse)
    tm = min(tm, s)
    o_specs = [pl.BlockSpec((1, dil, hb, tm // dil, LANES), lambda bi, i: (bi, 0, 0, i, 0))
               for dil in dils]
    l_specs = [pl.BlockSpec((1, dil, tm // dil, LANES), lambda bi, i: (bi, 0, i, 0)) for dil in dils]
    tile = pl.BlockSpec((1, tm, d), lambda bi, i: (bi, i, 0))
    expand = (jnp.arange(LANES)[:, None] == jnp.arange(he)[None, :] // DIL_HEAD_DIM).astype(BF16)
    return pl.pallas_call(
        functools.partial(_dil_out_kernel, alpha=alpha, dils=dils),
        grid=(bsz, s // tm),
        in_specs=o_specs + l_specs + [tile, _full((LANES, he)), _resident((he, d)),
                                      _full((1, d)), _full((1, d))],
        out_specs=tile,
        out_shape=jax.ShapeDtypeStruct((bsz, s, d), F32),
        scratch_shapes=[pltpu.VMEM((len(dils), hb, tm, LANES), F32),
                        pltpu.VMEM((len(dils), tm, LANES), F32)],
        compiler_params=_cparams("parallel", "parallel"),
        name="dil_combine_out",
    )(*outs, *lses, x, expand, w_out, _row(g), _row(b))


def _rwkv_prep_kernel(x_ref, xh_ref, mu_ref, wrkv_ref, w0_ref, w1_ref, w2_ref, a0_ref, a1_ref,
                      a2_ref, g1_ref, g2_ref, kk_ref, ka_ref,
                      r_o, k_o, v_o, lw_o, kk_o, b_o, g_o):
    i = pl.program_id(1)
    x = x_ref[0]
    tm = x.shape[0]
    prev_row = jnp.where(i == 0, 0.0, xh_ref[0, SUBLANES - 1:SUBLANES, :])
    rows = lax.broadcasted_iota(jnp.int32, (tm, 1), 0)
    xprev = jnp.where(rows == 0, prev_row, pltpu.roll(x, 1, 0))
    xx = xprev - x

    def mix(j):
        return x + xx * mu_ref[j:j + 1, :]

    r = _mm(mix(0), wrkv_ref[0])
    k = _mm(mix(1), wrkv_ref[1])
    v = _mm(mix(2), wrkv_ref[2])
    wl = w0_ref[...] + _mm(jnp.tanh(_mm(mix(3), w1_ref[...])), w2_ref[...])
    w_log = -_softplus(-wl) - 0.5
    a = _sigmoid(a0_ref[...] + _mm(_mm(mix(4), a1_ref[...]), a2_ref[...]))
    g = _mm(_sigmoid(_mm(mix(5), g1_ref[...])), g2_ref[...])
    kk = k * kk_ref[...]
    r_o[0] = r
    k_o[0] = k * (1.0 + (a - 1.0) * ka_ref[...])
    v_o[0] = v
    lw_o[0] = -jnp.exp(w_log)
    kk_o[0] = kk
    b_o[0] = kk * a
    g_o[0] = g


def _tri_inv(mats, blk):
    n = mats[0].shape[0]
    r = lax.broadcasted_iota(jnp.int32, (n, n), 0)
    c = lax.broadcasted_iota(jnp.int32, (n, n), 1)

    def same(bs):
        sh = int(math.log2(bs))
        return lax.shift_right_logical(r, sh) == lax.shift_right_logical(c, sh)

    base = same(TRI_BASE)
    eye = jnp.where(r == c, 1.0, 0.0)
    ps = [jnp.where(base, a, 0.0) for a in mats]
    ts = [eye + p for p in ps]
    ps = [_mm(p, p) for p in ps]
    m = 2
    while 2 * m < TRI_BASE:
        both = [_mm(p, jnp.concatenate([p, t], axis=1)) for p, t in zip(ps, ts)]
        ts = [t + x[:, n:] for t, x in zip(ts, both)]
        ps = [x[:, :n] for x in both]
        m *= 2
    ts = [t + _mm(p, t) for p, t in zip(ps, ts)]
    bs = TRI_BASE
    while bs < blk:
        sel = same(2 * bs) & jnp.logical_not(same(bs))
        nblk = n // bs
        rows = lambda z, i: z[i * bs:(i + 1) * bs]
        t_odd = [jnp.concatenate([rows(t, i) for i in range(1, nblk, 2)], axis=0) for t in ts]
        lo = [_mm(to, jnp.where(sel, a, 0.0)) for to, a in zip(t_odd, mats)]
        upd = [_mm(l, t) for l, t in zip(lo, ts)]
        ts = [jnp.concatenate([rows(t, i) + rows(u, i // 2) if i % 2 else rows(t, i)
                               for i in range(nblk)], axis=0)
              for t, u in zip(ts, upd)]
        bs *= 2
    return ts


def _rwkv_scan_kernel(r_ref, k_ref, v_ref, lw_ref, kk_ref, b_ref, lg_ref, lb_ref, rk_ref,
                      y_ref, st_ref, *, chunk):
    @pl.when(pl.program_id(2) == 0)
    def _():
        st_ref[...] = jnp.zeros_like(st_ref)

    L = chunk
    n2 = 2 * L
    tb, width = r_ref.shape[1], r_ref.shape[2]
    chunks = range(tb // L)
    pairs = range(width // LANES)
    probs = [(p, ch) for ch in chunks for p in pairs]
    sh = int(math.log2(L))
    lane = lax.broadcasted_iota(jnp.int32, (L, LANES), 1)
    head0 = lane < RWKV_HEAD
    rr = lax.broadcasted_iota(jnp.int32, (n2, n2), 0)
    cc = lax.broadcasted_iota(jnp.int32, (n2, n2), 1)
    rt, ct = rr & (L - 1), cc & (L - 1)
    strict, incl, eye = rt > ct, rt >= ct, rr == cc
    tr = lax.broadcasted_iota(jnp.int32, (tb, tb), 0)
    tc = lax.broadcasted_iota(jnp.int32, (tb, tb), 1)
    same_chunk = lax.shift_right_logical(tr, sh) == lax.shift_right_logical(tc, sh)
    head0_tb = lax.broadcasted_iota(jnp.int32, (tb, LANES), 1) < RWKV_HEAD

    def head_sum(z):
        out = []
        for p in pairs:
            zp = z[:, p * LANES:(p + 1) * LANES]
            s0 = jnp.sum(jnp.where(head0_tb, zp, 0.0), axis=-1, keepdims=True)
            s1 = jnp.sum(jnp.where(head0_tb, 0.0, zp), axis=-1, keepdims=True)
            out.append(jnp.where(head0_tb, s0, s1))
        return jnp.concatenate(out, axis=1)

    r, k, v = r_ref[0], k_ref[0], v_ref[0]
    lw, kk = lw_ref[0], kk_ref[0]
    inv_norm = 1.0 / jnp.maximum(jnp.sqrt(head_sum(kk * kk)), 1e-12)
    kn = kk * inv_norm
    b = b_ref[0] * inv_norm
    tri = jnp.where(same_chunk & (tr >= tc), 1.0, 0.0).astype(BF16)
    c = _mm_exact_lhs(tri, lw)
    c_end = jnp.concatenate(
        [jnp.broadcast_to(c[(ch + 1) * L - 1:(ch + 1) * L, :], (L, width)) for ch in chunks], axis=0)
    e_inv = jnp.exp(-c)
    e_end = jnp.exp(c_end - c)
    p_end = jnp.exp(c_end)
    a_t = -kn * jnp.exp(c - lw)
    r_t = r * jnp.exp(c)
    bh, kh = b * e_inv, k * e_inv
    bb, kb = b * e_end, k * e_end

    def stack(z, prob):
        p, ch = prob
        z = z[ch * L:(ch + 1) * L, p * LANES:(p + 1) * LANES]
        return jnp.concatenate([jnp.where(head0, z, 0.0), jnp.where(head0, 0.0, z)], axis=0)

    a_s = [stack(a_t, q) for q in probs]
    r_s = [stack(r_t, q) for q in probs]
    v_s = [stack(v, q) for q in probs]
    ms = [_mm_nt(jnp.concatenate([a, rs], axis=0),
                 jnp.concatenate([stack(bh, q), stack(kh, q)], axis=0))
          for a, rs, q in zip(a_s, r_s, probs)]
    ts = _tri_inv([jnp.where(strict, m[:n2, :n2], 0.0) for m in ms], L)
    av = [_mm(jnp.where(strict, m[:n2, n2:], 0.0), vs) for m, vs in zip(ms, v_s)]
    tu = [_mm(t, jnp.concatenate([a, x], axis=1)) for t, a, x in zip(ts, a_s, av)]
    zeros = jnp.zeros((n2, LANES), F32)
    incl2 = jnp.concatenate([incl, incl], axis=1)
    big = [_mm(jnp.concatenate([jnp.where(incl2, m[n2:], 0.0),
                                jnp.concatenate([stack(bb, q), stack(kb, q)], axis=0).T], axis=0),
               jnp.concatenate([x, jnp.concatenate([zeros, vs], axis=1)], axis=0))
           for m, q, x, vs in zip(ms, probs, tu, v_s)]
    rg_lhs, y0, h0 = {}, {}, {}
    for i, (p, ch) in enumerate(probs):
        p_last = p_end[ch * L:ch * L + 1, p * LANES:(p + 1) * LANES]
        rg_lhs[p, ch] = big[i][:, :LANES] + jnp.concatenate(
            [r_s[i], jnp.where(eye, p_last, 0.0)], axis=0)
        y0[p, ch] = big[i][:n2, LANES:]
        h0[p, ch] = big[i][n2:, LANES:]

    st = [st_ref[p] for p in pairs]
    ys = [[] for _ in pairs]
    for ch in chunks:
        for p in pairs:
            rg = _mm(rg_lhs[p, ch], st[p])
            y2 = rg[:n2] + y0[p, ch]
            st[p] = rg[n2:] + h0[p, ch]
            ys[p].append(y2[:L] + y2[L:])
    for p in pairs:
        st_ref[p] = st[p]

    y = jnp.concatenate([jnp.concatenate(yp, axis=0) for yp in ys], axis=1)
    inv_n = 1.0 / RWKV_HEAD
    yc = y - head_sum(y) * inv_n
    var = head_sum(yc * yc) * inv_n
    yn = yc * lax.rsqrt(var + RWKV_GN_EPS) * lg_ref[...] + lb_ref[...]
    y_ref[0] = yn + head_sum(r * k * rk_ref[...]) * v


def _rwkv_post_kernel(y_ref, gate_ref, x_ref, wo_ref, g_ref, b_ref, o_ref, *, alpha):
    hout = _mm(y_ref[...] * gate_ref[...], wo_ref[...])
    o_ref[...] = _postnorm(alpha, x_ref[...], hout, g_ref[...], b_ref[...])


def _rwkv_mixer(x, mu, w_rkv, w0, w1, w2, a0, a1, a2, g1, g2, k_k, k_a, r_k, lnx_g, lnx_b,
                w_out, g, b, alpha, tm=256, tblk=256):
    bsz, s, d = x.shape
    tm = min(tm, s)
    tile = pl.BlockSpec((1, tm, d), lambda bi, i: (bi, i, 0))
    hb = tm // SUBLANES
    lora = w1.shape[1]
    glora = g1.shape[1]
    outs = pl.pallas_call(
        _rwkv_prep_kernel,
        grid=(bsz, s // tm),
        in_specs=[tile,
                  pl.BlockSpec((1, SUBLANES, d), lambda bi, i: (bi, jnp.maximum(i * hb - 1, 0), 0)),
                  _full((6, d)), _resident((3, d, d)), _full((1, d)),
                  _full((d, lora)), _full((lora, d)), _full((1, d)),
                  _full((d, a1.shape[1])), _full((a2.shape[0], d)),
                  _full((d, glora)), _full((glora, d)),
                  _full((1, d)), _full((1, d))],
        out_specs=[tile] * 7,
        out_shape=[jax.ShapeDtypeStruct((bsz, s, d), F32)] * 7,
        compiler_params=_cparams("parallel", "parallel"),
        name="rwkv_prep",
    )(x, x, mu, w_rkv, _row(w0), w1, w2, _row(a0), a1, a2, g1, g2, _row(k_k), _row(k_a))
    r, k, v, lw, kk, bvec, gate = outs

    tblk = min(tblk, s)
    width = SCAN_PAIRS * LANES
    blk = pl.BlockSpec((1, tblk, width), lambda bi, hp, i: (bi, i, hp))
    chan = pl.BlockSpec((1, width), lambda bi, hp, i: (0, hp))
    y = pl.pallas_call(
        functools.partial(_rwkv_scan_kernel, chunk=RWKV_CHUNK),
        grid=(bsz, d // width, s // tblk),
        in_specs=[blk] * 6 + [chan] * 3,
        out_specs=blk,
        out_shape=jax.ShapeDtypeStruct((bsz, s, d), F32),
        scratch_shapes=[pltpu.VMEM((SCAN_PAIRS, LANES, LANES), F32)],
        compiler_params=_cparams("parallel", "parallel", "arbitrary"),
        name="rwkv_scan",
    )(r, k, v, lw, kk, bvec, _row(lnx_g), _row(lnx_b), _row(r_k))

    n = bsz * s
    tm2 = min(tm, n)
    row = pl.BlockSpec((tm2, d), lambda i: (i, 0))
    flat = lambda z: z.reshape(n, d)
    out = pl.pallas_call(
        functools.partial(_rwkv_post_kernel, alpha=alpha),
        grid=(n // tm2,),
        in_specs=[row] * 3 + [_resident((d, d)), _full((1, d)), _full((1, d))],
        out_specs=row,
        out_shape=jax.ShapeDtypeStruct((n, d), F32),
        compiler_params=_cparams("parallel"),
        name="rwkv_post",
    )(flat(y), flat(gate), flat(x), w_out, _row(g), _row(b))
    return out.reshape(bsz, s, d)


def kernel(x, mem, rel_bias, a_w_in, a_b_in, a_dw, a_dw_b, a_ln_g, a_ln_b, a_w_out, a_b_out, b_w_qkv, b_w_out, c_mu, c_w_rkv, c_w0, c_w1, c_w2, c_a0, c_a1, c_a2, c_g1, c_g2, c_k_k, c_k_a, c_r_k, c_lnx_g, c_lnx_b, c_w_out, x_w_q, x_w_kv, x_w_out, m_w1, m_w2, ln_g, ln_b):
    depth = ln_g.shape[0]
    alpha = (2 * depth) ** 0.25
    bsz, s, d = x.shape
    mlen = mem.shape[1]
    w = lambda t: t.astype(MXU_DTYPE)
    mem2d = mem.reshape(bsz * mlen, d)
    for i in range(depth):
        kind, j = i % 3, i // 3
        if kind == 0:
            x = _conv_mixer(x, a_w_in[j], a_b_in[j], a_dw[j], a_dw_b[j], a_ln_g[j], a_ln_b[j],
                            w(a_w_out[j]), a_b_out[j], ln_g[i, 0], ln_b[i, 0], alpha)
        elif kind == 1:
            x = _dil_mixer(x, b_w_qkv[j], b_w_out[j], rel_bias, ln_g[i, 0], ln_b[i, 0], alpha)
        else:
            x = _rwkv_mixer(x, c_mu[j], c_w_rkv[j], c_w0[j], c_w1[j], c_w2[j], c_a0[j],
                            c_a1[j], c_a2[j], c_g1[j], c_g2[j], c_k_k[j], c_k_a[j],
                            c_r_k[j], c_lnx_g[j], c_lnx_b[j], c_w_out[j],
                            ln_g[i, 0], ln_b[i, 0], alpha)
        kv = _linear(mem2d, x_w_kv[i], MXU_DTYPE, 512, 1024).reshape(bsz, mlen, 2 * d)
        x = _cross_attention(x, kv, x_w_q[i], x_w_out[i], ln_g[i, 1], ln_b[i, 1], alpha)
        x = _mlp(x.reshape(bsz * s, d), w(m_w1[i]), w(m_w2[i]), ln_g[i, 2], ln_b[i, 2],
                 alpha).reshape(bsz, s, d)
    return x
```

```python
import functools
import math

import jax
import jax.numpy as jnp
from jax import lax
from jax.experimental import pallas as pl
from jax.experimental.pallas import tpu as pltpu

F32 = jnp.float32
BF16 = jnp.bfloat16
MXU_DTYPE = jnp.bfloat16

CONV_WIDTH = 31
DIL_GROUPS = ((128, 1), (512, 4), (2048, 16))
DIL_HEADS = 16
DIL_HEAD_DIM = 64
REL_BUCKETS = 32
REL_MAX_DIST = 2048
RWKV_HEAD = 64
RWKV_GN_EPS = 64e-5
XATTN_HEADS = 4
LN_EPS = 1e-5
NEG_BIG = -1e30

LANES = 128
SUBLANES = 8
VMEM_LIMIT_BYTES = 56 * 1024 * 1024

RWKV_CHUNK = 64
TRI_BASE = 8
SCAN_PAIRS = 4


def _cparams(*sem):
    return pltpu.CompilerParams(dimension_semantics=sem, vmem_limit_bytes=VMEM_LIMIT_BYTES)


def _mm(a, b):
    return jnp.dot(a.astype(MXU_DTYPE), b.astype(MXU_DTYPE), preferred_element_type=F32)


def _mm_nt(a, b):
    return lax.dot_general(a.astype(MXU_DTYPE), b.astype(MXU_DTYPE),
                           (((1,), (1,)), ((), ())), preferred_element_type=F32)


def _split3(x):
    hi = x.astype(BF16)
    r1 = x - hi.astype(F32)
    mid = r1.astype(BF16)
    lo = (r1 - mid.astype(F32)).astype(BF16)
    return hi, mid, lo


def _mm_exact_rhs(x, e, passes=3):
    pieces = _split3(x)[:passes]
    return sum(jnp.dot(p, e, preferred_element_type=F32) for p in pieces)


def _mm_exact_lhs(e, x):
    hi, mid, lo = _split3(x)
    dot = functools.partial(jnp.dot, preferred_element_type=F32)
    return dot(e, hi) + dot(e, mid) + dot(e, lo)


def _sigmoid(z):
    return 1.0 / (1.0 + jnp.exp(-z))


def _softplus(z):
    return jnp.maximum(z, 0.0) + jnp.log(1.0 + jnp.exp(-jnp.abs(z)))


def _layer_norm(z, g, b, eps=LN_EPS):
    mu = jnp.mean(z, axis=-1, keepdims=True)
    zc = z - mu
    var = jnp.mean(zc * zc, axis=-1, keepdims=True)
    return zc * lax.rsqrt(var + eps) * g + b


def _postnorm(alpha, x, h, g, b):
    return _layer_norm(alpha * x + h, g, b)


def _full(shape):
    nd = len(shape)
    return pl.BlockSpec(shape, lambda *_: (0,) * nd)


def _resident(shape):
    nd = len(shape)
    return pl.BlockSpec(shape, lambda *_: (0,) * nd, pipeline_mode=pl.Buffered(1))


def _layer(arr, layer):
    nd = arr.ndim - 1
    return pl.BlockSpec((None,) + arr.shape[1:], lambda *_: (layer,) + (0,) * nd,
                        pipeline_mode=pl.Buffered(1))


def _row(v):
    return v.reshape(1, -1)


def _linear_kernel(x_ref, w_ref, o_ref):
    o_ref[...] = _mm(x_ref[...], w_ref[...]).astype(o_ref.dtype)


def _linear(x2d, w, layer, out_dtype, tm, tn):
    m, k = x2d.shape
    n = w.shape[2]
    tm, tn = min(tm, m), min(tn, n)
    return pl.pallas_call(
        _linear_kernel,
        grid=(m // tm, n // tn),
        in_specs=[pl.BlockSpec((tm, k), lambda i, j: (i, 0)),
                  pl.BlockSpec((None, k, tn), lambda i, j: (layer, 0, j))],
        out_specs=pl.BlockSpec((tm, tn), lambda i, j: (i, j)),
        out_shape=jax.ShapeDtypeStruct((m, n), out_dtype),
        compiler_params=_cparams("parallel", "parallel"),
        name="linear",
    )(x2d, w)


def _xattn_kernel(x_ref, kv_ref, wq_ref, wo_ref, g_ref, b_ref, o_ref, oh_ref, *, heads, alpha):
    xb = x_ref[0]
    d = xb.shape[-1]
    e = d // heads
    q = (_mm(xb, wq_ref[...]) * (e ** -0.5)).astype(MXU_DTYPE)
    for h in range(heads):
        kh = kv_ref[0, :, h * e:(h + 1) * e]
        vh = kv_ref[0, :, d + h * e:d + (h + 1) * e]
        s = _mm_nt(q[:, h * e:(h + 1) * e], kh)
        m = jnp.max(s, axis=-1, keepdims=True)
        p = jnp.exp(s - m)
        den = jnp.sum(p, axis=-1, keepdims=True)
        oh_ref[:, h * e:(h + 1) * e] = _mm(p, vh) / den
    hout = _mm(oh_ref[...], wo_ref[...])
    o_ref[0] = _postnorm(alpha, xb, hout, g_ref[...], b_ref[...])


def _cross_attention(x, kv, wq, wo, layer, g, b, alpha, tm=1024):
    bsz, s, d = x.shape
    mlen = kv.shape[1]
    tm = min(tm, s)
    return pl.pallas_call(
        functools.partial(_xattn_kernel, heads=XATTN_HEADS, alpha=alpha),
        grid=(bsz, s // tm),
        in_specs=[pl.BlockSpec((1, tm, d), lambda bi, i: (bi, i, 0)),
                  pl.BlockSpec((1, mlen, 2 * d), lambda bi, i: (bi, 0, 0)),
                  _layer(wq, layer), _layer(wo, layer), _full((1, d)), _full((1, d))],
        out_specs=pl.BlockSpec((1, tm, d), lambda bi, i: (bi, i, 0)),
        out_shape=jax.ShapeDtypeStruct((bsz, s, d), F32),
        scratch_shapes=[pltpu.VMEM((tm, d), F32)],
        compiler_params=_cparams("parallel", "parallel"),
        name="cross_attention",
    )(x, kv, wq, wo, _row(g), _row(b))


def _mlp_kernel(x_ref, w1_ref, w2_ref, g_ref, b_ref, o_ref, *, fchunk, alpha):
    xb = x_ref[...]
    xm = xb.astype(MXU_DTYPE)
    ff = w1_ref.shape[1]
    acc = jnp.zeros(xb.shape, F32)
    for c in range(ff // fchunk):
        hid = _mm(xm, w1_ref[:, c * fchunk:(c + 1) * fchunk])
        hid = jnp.maximum(hid, 0.0)
        acc = acc + _mm(hid * hid, w2_ref[c * fchunk:(c + 1) * fchunk, :])
    o_ref[...] = _postnorm(alpha, xb, acc, g_ref[...], b_ref[...])


def _mlp(x2d, w1, w2, layer, g, b, alpha, tm=512, fchunk=1024):
    n, d = x2d.shape
    ff = w1.shape[2]
    tm = min(tm, n)
    return pl.pallas_call(
        functools.partial(_mlp_kernel, fchunk=min(fchunk, ff), alpha=alpha),
        grid=(n // tm,),
        in_specs=[pl.BlockSpec((tm, d), lambda i: (i, 0)),
                  _layer(w1, layer), _layer(w2, layer), _full((1, d)), _full((1, d))],
        out_specs=pl.BlockSpec((tm, d), lambda i: (i, 0)),
        out_shape=jax.ShapeDtypeStruct((n, d), F32),
        compiler_params=_cparams("parallel"),
        name="sq_relu_mlp",
    )(x2d, w1, w2, _row(g), _row(b))


def _glu_kernel(x_ref, w_ref, b_ref, o_ref):
    h = _mm(x_ref[...], w_ref[...]) + b_ref[...]
    d = o_ref.shape[-1]
    o_ref[...] = h[:, :d] * _sigmoid(h[:, d:])


def _glu_proj(x2d, w, layer, b, tm=512):
    n, d = x2d.shape
    tm = min(tm, n)
    return pl.pallas_call(
        _glu_kernel,
        grid=(n // tm,),
        in_specs=[pl.BlockSpec((tm, d), lambda i: (i, 0)), _layer(w, layer), _full((1, 2 * d))],
        out_specs=pl.BlockSpec((tm, d), lambda i: (i, 0)),
        out_shape=jax.ShapeDtypeStruct((n, d), F32),
        compiler_params=_cparams("parallel"),
        name="conv_glu_proj",
    )(x2d, w, _row(b))


CONV_HALO = 32


def _conv_kernel(u_ref, halo_ref, x_ref, dw_ref, dwb_ref, lng_ref, lnb_ref, wo_ref, bo_ref,
                 g_ref, b_ref, o_ref, cat_ref, sh_ref, *, alpha):
    i = pl.program_id(1)
    tm = u_ref.shape[1]
    cat_ref[0:CONV_HALO, :] = jnp.where(i == 0, 0.0, halo_ref[0])
    cat_ref[CONV_HALO:CONV_HALO + tm, :] = u_ref[0]
    acc = jnp.broadcast_to(dwb_ref[...], (tm, u_ref.shape[2]))
    off = CONV_HALO - (CONV_WIDTH - 1)
    for phase in range(SUBLANES):
        taps = [j for j in range(CONV_WIDTH) if (off + j) % SUBLANES == phase]
        span = max(off + j for j in taps) - phase + tm
        src = cat_ref
        if phase:
            sh_ref[0:span, :] = cat_ref[phase:phase + span, :]
            src = sh_ref
        for j in taps:
            a = off + j - phase
            acc = acc + dw_ref[j:j + 1, :] * src[a:a + tm, :]
    z = _layer_norm(acc, lng_ref[...], lnb_ref[...])
    z = z * _sigmoid(z)
    hout = _mm(z, wo_ref[...]) + bo_ref[...]
    o_ref[0] = _postnorm(alpha, x_ref[0], hout, g_ref[...], b_ref[...])


def _conv_mixer(x, w_in, layer, b_in, dw, dw_b, ln_g, ln_b, w_out, b_out, g, b, alpha, tm=256):
    bsz, s, d = x.shape
    u = _glu_proj(x.reshape(bsz * s, d), w_in, layer, b_in).reshape(bsz, s, d)
    tm = min(tm, s)
    hb = tm // CONV_HALO
    return pl.pallas_call(
        functools.partial(_conv_kernel, alpha=alpha),
        grid=(bsz, s // tm),
        in_specs=[pl.BlockSpec((1, tm, d), lambda bi, i: (bi, i, 0)),
                  pl.BlockSpec((1, CONV_HALO, d), lambda bi, i: (bi, jnp.maximum(i * hb - 1, 0), 0)),
                  pl.BlockSpec((1, tm, d), lambda bi, i: (bi, i, 0)),
                  _full((CONV_WIDTH, d)), _full((1, d)), _full((1, d)), _full((1, d)),
                  _layer(w_out, layer), _full((1, d)), _full((1, d)), _full((1, d))],
        out_specs=pl.BlockSpec((1, tm, d), lambda bi, i: (bi, i, 0)),
        out_shape=jax.ShapeDtypeStruct((bsz, s, d), F32),
        scratch_shapes=[pltpu.VMEM((CONV_HALO + tm, d), F32), pltpu.VMEM((CONV_HALO + tm, d), F32)],
        compiler_params=_cparams("parallel", "parallel"),
        name="conv_dw_ln_out",
    )(u, u, x, dw, _row(dw_b), _row(ln_g), _row(ln_b), w_out, _row(b_out), _row(g), _row(b))


def _t5_causal_bucket(dist):
    n = jnp.maximum(dist, 0)
    max_exact = REL_BUCKETS // 2
    nf = jnp.maximum(n, 1).astype(F32)
    large = max_exact + (jnp.log(nf / max_exact) / math.log(REL_MAX_DIST / max_exact)
                         * (REL_BUCKETS - max_exact)).astype(jnp.int32)
    large = jnp.minimum(large, REL_BUCKETS - 1)
    return jnp.where(n < max_exact, n, large)


def _qkv_kernel(x_ref, w_ref, o_ref, xs_ref, xp_ref, *, dil, qscale):
    tm = x_ref.shape[1]
    n = tm // dil
    if dil == 1:
        xp = x_ref[0]
    else:
        for j in range(xs_ref.shape[0]):
            cols = slice(j * LANES, (j + 1) * LANES)
            xs_ref[j] = x_ref[0, :, cols]
            for c in range(dil):
                xp_ref[c * n:(c + 1) * n, cols] = (
                    xs_ref[j, pl.ds(c, n, stride=dil), :].astype(xp_ref.dtype))
        xp = xp_ref[...]
    acc = _mm(xp, w_ref[...])
    nblk = o_ref.shape[2]
    for c in range(dil):
        for j in range(nblk):
            tile = acc[c * n:(c + 1) * n, j * LANES:(j + 1) * LANES]
            if 3 * j < nblk:
                tile = tile * qscale
            o_ref[0, c, j] = tile.astype(o_ref.dtype)


def _qkv_proj(x, w_qkv, layer, group, dil, tm=512):
    bsz, s, d = x.shape
    sub = s // dil
    ncol = w_qkv.shape[2] // len(DIL_GROUPS)
    nblk = ncol // LANES
    tm = min(tm, s)
    return pl.pallas_call(
        functools.partial(_qkv_kernel, dil=dil, qscale=DIL_HEAD_DIM ** -0.5),
        grid=(bsz, s // tm),
        in_specs=[pl.BlockSpec((1, tm, d), lambda bi, i: (bi, i, 0)),
                  pl.BlockSpec((None, d, ncol), lambda bi, i: (layer, 0, group),
                               pipeline_mode=pl.Buffered(1))],
        out_specs=pl.BlockSpec((1, dil, nblk, tm // dil, LANES), lambda bi, i: (bi, 0, 0, i, 0)),
        out_shape=jax.ShapeDtypeStruct((bsz, dil, nblk, sub, LANES), MXU_DTYPE),
        scratch_shapes=[pltpu.VMEM((d // LANES, tm, LANES), F32), pltpu.VMEM((tm, d), MXU_DTYPE)],
        compiler_params=_cparams("parallel", "parallel"),
        name="dil_qkv_proj",
    )(x, w_qkv)


def _dil_bias_kernel(bucket_ref, relb_ref, bias_ref, *, heads, blk):
    bkt = bucket_ref[...]
    kcol = lax.broadcasted_iota(jnp.int32, bkt.shape, 1)
    for h in range(heads):
        acc = jnp.full(bkt.shape, NEG_BIG, F32)
        for n in range(REL_BUCKETS):
            acc = jnp.where(bkt == n, relb_ref[n, h], acc)
        bias_ref[h] = acc
        bias_ref[heads + h] = jnp.where(kcol < blk, NEG_BIG, acc)


def _dil_attn_kernel(bias_ref, q_ref, k_ref, v_ref, kp_ref, vp_ref, o_ref, l_ref,
                     kc_ref, vc_ref, *, heads, blk):
    i = pl.program_id(2)
    tq = q_ref.shape[3]
    kc_ref[:, 0:blk, :] = kp_ref[0, 0]
    kc_ref[:, blk:blk + tq, :] = k_ref[0, 0]
    vc_ref[:, 0:blk, :] = vp_ref[0, 0]
    vc_ref[:, blk:blk + tq, :] = v_ref[0, 0]
    lane = lax.broadcasted_iota(jnp.int32, (blk, LANES), 1)
    low = lane < DIL_HEAD_DIM

    for jb in range(tq // blk):
        rows = slice(jb * blk, (jb + 1) * blk)
        keys = slice(jb * blk, (jb + 2) * blk)
        table = jnp.where(i == 0, heads, 0) if jb == 0 else 0

        def pair(hp, lse_all):
            q2 = q_ref[0, 0, hp, rows, :]
            k2 = kc_ref[hp, keys, :]
            v2 = vc_ref[hp, keys, :]
            outs = []
            for hh in range(2):
                h = 2 * hp + hh
                keep = low if hh == 0 else jnp.logical_not(low)
                qm = jnp.where(keep, q2, jnp.zeros_like(q2))
                s = _mm_nt(qm, k2) + bias_ref[table + h]
                m = jnp.max(s, axis=-1, keepdims=True)
                p = jnp.exp(s - m)
                den = jnp.sum(p, axis=-1, keepdims=True)
                outs.append(_mm(p, v2) / den)
                lse_all = jnp.where(lane == h, m + jnp.log(den), lse_all)
            o_ref[0, 0, hp, rows, :] = jnp.where(low, outs[0], outs[1]).astype(o_ref.dtype)
            return lse_all

        lse_all = jnp.zeros((blk, LANES), F32)
        for hp in range(heads // 2):
            lse_all = pair(hp, lse_all)
        l_ref[0, 0, rows, :] = lse_all


def _dil_attention_group(qkv, rel_bias, window, dil, tq=512):
    bsz, _, nblk, sub, _ = qkv.shape
    hb = nblk // 3
    blk = window // dil
    tq = min(tq, sub)
    nsub = tq // blk
    qi = jnp.arange(blk)[:, None]
    kj = jnp.arange(2 * blk)[None, :]
    rel = qi + blk - kj
    bucket = jnp.where((rel >= 0) & (rel <= blk), _t5_causal_bucket(rel * dil), -1).astype(jnp.int32)

    bias = pl.pallas_call(
        functools.partial(_dil_bias_kernel, heads=DIL_HEADS, blk=blk),
        in_specs=[pl.BlockSpec(memory_space=pltpu.VMEM), pl.BlockSpec(memory_space=pltpu.SMEM)],
        out_specs=pl.BlockSpec(memory_space=pltpu.VMEM),
        out_shape=jax.ShapeDtypeStruct((2 * DIL_HEADS, blk, 2 * blk), F32),
        name="dil_bias_table",
    )(bucket, rel_bias)

    def cur(which):
        return pl.BlockSpec((1, 1, hb, tq, LANES), lambda bi, c, i: (bi, c, which, i, 0))

    def prev(which):
        return pl.BlockSpec((1, 1, hb, blk, LANES),
                            lambda bi, c, i: (bi, c, which, jnp.maximum(i * nsub - 1, 0), 0))

    return pl.pallas_call(
        functools.partial(_dil_attn_kernel, heads=DIL_HEADS, blk=blk),
        grid=(bsz, dil, sub // tq),
        in_specs=[_resident((2 * DIL_HEADS, blk, 2 * blk)), cur(0), cur(1), cur(2), prev(1), prev(2)],
        out_specs=[pl.BlockSpec((1, 1, hb, tq, LANES), lambda bi, c, i: (bi, c, 0, i, 0)),
                   pl.BlockSpec((1, 1, tq, LANES), lambda bi, c, i: (bi, c, i, 0))],
        out_shape=[jax.ShapeDtypeStruct((bsz, dil, hb, sub, LANES), MXU_DTYPE),
                   jax.ShapeDtypeStruct((bsz, dil, sub, LANES), F32)],
        scratch_shapes=[pltpu.VMEM((hb, blk + tq, LANES), MXU_DTYPE),
                        pltpu.VMEM((hb, blk + tq, LANES), MXU_DTYPE)],
        compiler_params=_cparams("parallel", "parallel", "parallel"),
        name="dil_attention",
    )(bias, qkv, qkv, qkv, qkv, qkv)


def _dil_out_kernel(o1_ref, o2_ref, o3_ref, l1_ref, l2_ref, l3_ref, x_ref, e_ref, wo_ref, g_ref,
                    b_ref, out_ref, os_ref, ls_ref, *, alpha, dils):
    tm = x_ref.shape[1]
    o_refs, l_refs = (o1_ref, o2_ref, o3_ref), (l1_ref, l2_ref, l3_ref)

    def natural(dst_ref, src, dil):
        if dil == 1:
            return src(0).astype(F32)
        n = tm // dil
        for c in range(dil):
            dst_ref[pl.ds(c, n, stride=dil), :] = src(c).astype(F32)
        return dst_ref[...]

    l1, l2, l3 = [natural(ls_ref.at[g], lambda c, r=l_refs[g]: r[0, c], dils[g]) for g in range(3)]
    m = jnp.maximum(jnp.maximum(l1, l2), l3)
    es = [jnp.exp(l1 - m), jnp.exp(l2 - m), jnp.exp(l3 - m)]
    inv = 1.0 / (es[0] + es[1] + es[2])
    ws = [_mm_exact_rhs(e * inv, e_ref[...], passes=2) for e in es]
    cols = []
    for j in range(o1_ref.shape[2]):
        blk = slice(j * LANES, (j + 1) * LANES)
        cols.append(sum(ws[g][:, blk] * natural(os_ref.at[g, j],
                                                lambda c, r=o_refs[g], j=j: r[0, c, j], dils[g])
                        for g in range(3)))
    hout = _mm(jnp.concatenate(cols, axis=1), wo_ref[...])
    out_ref[0] = _postnorm(alpha, x_ref[0], hout, g_ref[...], b_ref[...])


def _dil_mixer(x, w_qkv, w_out, layer, rel_bias, g, b, alpha, tm=256):
    bsz, s, d = x.shape
    he = DIL_HEADS * DIL_HEAD_DIM
    hb = he // LANES
    dils = tuple(dil for _, dil in DIL_GROUPS)
    outs, lses = [], []
    for gi, (window, dil) in enumerate(DIL_GROUPS):
        qkv = _qkv_proj(x, w_qkv, layer, gi, dil)
        o, lse = _dil_attention_group(qkv, rel_bias, window, dil)
        outs.append(o)
        lses.append(lse)
    tm = min(tm, s)
    o_specs = [pl.BlockSpec((1, dil, hb, tm // dil, LANES), lambda bi, i: (bi, 0, 0, i, 0))
               for dil in dils]
    l_specs = [pl.BlockSpec((1, dil, tm // dil, LANES), lambda bi, i: (bi, 0, i, 0)) for dil in dils]
    tile = pl.BlockSpec((1, tm, d), lambda bi, i: (bi, i, 0))
    expand = (jnp.arange(LANES)[:, None] == jnp.arange(he)[None, :] // DIL_HEAD_DIM).astype(BF16)
    return pl.pallas_call(
        functools.partial(_dil_out_kernel, alpha=alpha, dils=dils),
        grid=(bsz, s // tm),
        in_specs=o_specs + l_specs + [tile, _full((LANES, he)), _layer(w_out, layer),
                                      _full((1, d)), _full((1, d))],
        out_specs=tile,
        out_shape=jax.ShapeDtypeStruct((bsz, s, d), F32),
        scratch_shapes=[pltpu.VMEM((len(dils), hb, tm, LANES), F32),
                        pltpu.VMEM((len(dils), tm, LANES), F32)],
        compiler_params=_cparams("parallel", "parallel"),
        name="dil_combine_out",
    )(*outs, *lses, x, expand, w_out, _row(g), _row(b))


def _rwkv_prep_kernel(x_ref, xh_ref, mu_ref, wrkv_ref, w0_ref, w1_ref, w2_ref, a0_ref, a1_ref,
                      a2_ref, g1_ref, g2_ref, kk_ref, ka_ref,
                      r_o, k_o, v_o, lw_o, kk_o, b_o, g_o):
    i = pl.program_id(1)
    x = x_ref[0]
    tm = x.shape[0]
    prev_row = jnp.where(i == 0, 0.0, xh_ref[0, SUBLANES - 1:SUBLANES, :])
    rows = lax.broadcasted_iota(jnp.int32, (tm, 1), 0)
    xprev = jnp.where(rows == 0, prev_row, pltpu.roll(x, 1, 0))
    xx = xprev - x

    def mix(j):
        return x + xx * mu_ref[j:j + 1, :]

    r = _mm(mix(0), wrkv_ref[0])
    k = _mm(mix(1), wrkv_ref[1])
    v = _mm(mix(2), wrkv_ref[2])
    wl = w0_ref[...] + _mm(jnp.tanh(_mm(mix(3), w1_ref[...])), w2_ref[...])
    w_log = -_softplus(-wl) - 0.5
    a = _sigmoid(a0_ref[...] + _mm(_mm(mix(4), a1_ref[...]), a2_ref[...]))
    g = _mm(_sigmoid(_mm(mix(5), g1_ref[...])), g2_ref[...])
    kk = k * kk_ref[...]
    r_o[0] = r
    k_o[0] = k * (1.0 + (a - 1.0) * ka_ref[...])
    v_o[0] = v
    lw_o[0] = -jnp.exp(w_log)
    kk_o[0] = kk
    b_o[0] = kk * a
    g_o[0] = g


def _tri_inv(mats, blk):
    n = mats[0].shape[0]
    r = lax.broadcasted_iota(jnp.int32, (n, n), 0)
    c = lax.broadcasted_iota(jnp.int32, (n, n), 1)

    def same(bs):
        sh = int(math.log2(bs))
        return lax.shift_right_logical(r, sh) == lax.shift_right_logical(c, sh)

    base = same(TRI_BASE)
    eye = jnp.where(r == c, 1.0, 0.0)
    ps = [jnp.where(base, a, 0.0) for a in mats]
    ts = [eye + p for p in ps]
    ps = [_mm(p, p) for p in ps]
    m = 2
    while 2 * m < TRI_BASE:
        both = [_mm(p, jnp.concatenate([p, t], axis=1)) for p, t in zip(ps, ts)]
        ts = [t + x[:, n:] for t, x in zip(ts, both)]
        ps = [x[:, :n] for x in both]
        m *= 2
    ts = [t + _mm(p, t) for p, t in zip(ps, ts)]
    bs = TRI_BASE
    while bs < blk:
        sel = same(2 * bs) & jnp.logical_not(same(bs))
        nblk = n // bs
        rows = lambda z, i: z[i * bs:(i + 1) * bs]
        t_odd = [jnp.concatenate([rows(t, i) for i in range(1, nblk, 2)], axis=0) for t in ts]
        lo = [_mm(to, jnp.where(sel, a, 0.0)) for to, a in zip(t_odd, mats)]
        upd = [_mm(l, t) for l, t in zip(lo, ts)]
        ts = [jnp.concatenate([rows(t, i) + rows(u, i // 2) if i % 2 else rows(t, i)
                               for i in range(nblk)], axis=0)
              for t, u in zip(ts, upd)]
        bs *= 2
    return ts


def _rwkv_scan_kernel(r_ref, k_ref, v_ref, lw_ref, kk_ref, b_ref, lg_ref, lb_ref, rk_ref,
                      y_ref, st_ref, *, chunk):
    @pl.when(pl.program_id(2) == 0)
    def _():
        st_ref[...] = jnp.zeros_like(st_ref)

    L = chunk
    n2 = 2 * L
    tb, width = r_ref.shape[1], r_ref.shape[2]
    chunks = range(tb // L)
    pairs = range(width // LANES)
    probs = [(p, ch) for ch in chunks for p in pairs]
    sh = int(math.log2(L))
    lane = lax.broadcasted_iota(jnp.int32, (L, LANES), 1)
    head0 = lane < RWKV_HEAD
    rr = lax.broadcasted_iota(jnp.int32, (n2, n2), 0)
    cc = lax.broadcasted_iota(jnp.int32, (n2, n2), 1)
    rt, ct = rr & (L - 1), cc & (L - 1)
    strict, incl, eye = rt > ct, rt >= ct, rr == cc
    tr = lax.broadcasted_iota(jnp.int32, (tb, tb), 0)
    tc = lax.broadcasted_iota(jnp.int32, (tb, tb), 1)
    same_chunk = lax.shift_right_logical(tr, sh) == lax.shift_right_logical(tc, sh)
    head0_tb = lax.broadcasted_iota(jnp.int32, (tb, LANES), 1) < RWKV_HEAD

    def head_sum(z):
        out = []
        for p in pairs:
            zp = z[:, p * LANES:(p + 1) * LANES]
            s0 = jnp.sum(jnp.where(head0_tb, zp, 0.0), axis=-1, keepdims=True)
            s1 = jnp.sum(jnp.where(head0_tb, 0.0, zp), axis=-1, keepdims=True)
            out.append(jnp.where(head0_tb, s0, s1))
        return jnp.concatenate(out, axis=1)

    r, k, v = r_ref[0], k_ref[0], v_ref[0]
    lw, kk = lw_ref[0], kk_ref[0]
    inv_norm = 1.0 / jnp.maximum(jnp.sqrt(head_sum(kk * kk)), 1e-12)
    kn = kk * inv_norm
    b = b_ref[0] * inv_norm
    tri = jnp.where(same_chunk & (tr >= tc), 1.0, 0.0).astype(BF16)
    c = _mm_exact_lhs(tri, lw)
    c_end = jnp.concatenate(
        [jnp.broadcast_to(c[(ch + 1) * L - 1:(ch + 1) * L, :], (L, width)) for ch in chunks], axis=0)
    e_inv = jnp.exp(-c)
    e_end = jnp.exp(c_end - c)
    p_end = jnp.exp(c_end)
    a_t = -kn * jnp.exp(c - lw)
    r_t = r * jnp.exp(c)
    bh, kh = b * e_inv, k * e_inv
    bb, kb = b * e_end, k * e_end

    def stack(z, prob):
        p, ch = prob
        z = z[ch * L:(ch + 1) * L, p * LANES:(p + 1) * LANES]
        return jnp.concatenate([jnp.where(head0, z, 0.0), jnp.where(head0, 0.0, z)], axis=0)

    a_s = [stack(a_t, q) for q in probs]
    r_s = [stack(r_t, q) for q in probs]
    v_s = [stack(v, q) for q in probs]
    ms = [_mm_nt(jnp.concatenate([a, rs], axis=0),
                 jnp.concatenate([stack(bh, q), stack(kh, q)], axis=0))
          for a, rs, q in zip(a_s, r_s, probs)]
    ts = _tri_inv([jnp.where(strict, m[:n2, :n2], 0.0) for m in ms], L)
    av = [_mm(jnp.where(strict, m[:n2, n2:], 0.0), vs) for m, vs in zip(ms, v_s)]
    tu = [_mm(t, jnp.concatenate([a, x], axis=1)) for t, a, x in zip(ts, a_s, av)]
    zeros = jnp.zeros((n2, LANES), F32)
    incl2 = jnp.concatenate([incl, incl], axis=1)
    big = [_mm(jnp.concatenate([jnp.where(incl2, m[n2:], 0.0),
                                jnp.concatenate([stack(bb, q), stack(kb, q)], axis=0).T], axis=0),
               jnp.concatenate([x, jnp.concatenate([zeros, vs], axis=1)], axis=0))
           for m, q, x, vs in zip(ms, probs, tu, v_s)]
    rg_lhs, y0, h0 = {}, {}, {}
    for i, (p, ch) in enumerate(probs):
        p_last = p_end[ch * L:ch * L + 1, p * LANES:(p + 1) * LANES]
        rg_lhs[p, ch] = big[i][:, :LANES] + jnp.concatenate(
            [r_s[i], jnp.where(eye, p_last, 0.0)], axis=0)
        y0[p, ch] = big[i][:n2, LANES:]
        h0[p, ch] = big[i][n2:, LANES:]

    st = [st_ref[p] for p in pairs]
    ys = [[] for _ in pairs]
    for ch in chunks:
        for p in pairs:
            rg = _mm(rg_lhs[p, ch], st[p])
            y2 = rg[:n2] + y0[p, ch]
            st[p] = rg[n2:] + h0[p, ch]
            ys[p].append(y2[:L] + y2[L:])
    for p in pairs:
        st_ref[p] = st[p]

    y = jnp.concatenate([jnp.concatenate(yp, axis=0) for yp in ys], axis=1)
    inv_n = 1.0 / RWKV_HEAD
    yc = y - head_sum(y) * inv_n
    var = head_sum(yc * yc) * inv_n
    yn = yc * lax.rsqrt(var + RWKV_GN_EPS) * lg_ref[...] + lb_ref[...]
    y_ref[0] = yn + head_sum(r * k * rk_ref[...]) * v


def _rwkv_post_kernel(y_ref, gate_ref, x_ref, wo_ref, g_ref, b_ref, o_ref, *, alpha):
    hout = _mm(y_ref[...] * gate_ref[...], wo_ref[...])
    o_ref[...] = _postnorm(alpha, x_ref[...], hout, g_ref[...], b_ref[...])


def _rwkv_mixer(x, layer, mu, w_rkv, w0, w1, w2, a0, a1, a2, g1, g2, k_k, k_a, r_k, lnx_g, lnx_b,
                w_out, g, b, alpha, tm=256, tblk=256):
    bsz, s, d = x.shape
    tm = min(tm, s)
    tile = pl.BlockSpec((1, tm, d), lambda bi, i: (bi, i, 0))
    hb = tm // SUBLANES
    outs = pl.pallas_call(
        _rwkv_prep_kernel,
        grid=(bsz, s // tm),
        in_specs=[tile,
                  pl.BlockSpec((1, SUBLANES, d), lambda bi, i: (bi, jnp.maximum(i * hb - 1, 0), 0)),
                  _full((6, d)), _layer(w_rkv, layer), _full((1, d)),
                  _layer(w1, layer), _layer(w2, layer), _full((1, d)),
                  _layer(a1, layer), _layer(a2, layer),
                  _layer(g1, layer), _layer(g2, layer),
                  _full((1, d)), _full((1, d))],
        out_specs=[tile] * 7,
        out_shape=[jax.ShapeDtypeStruct((bsz, s, d), F32)] * 7,
        compiler_params=_cparams("parallel", "parallel"),
        name="rwkv_prep",
    )(x, x, mu, w_rkv, _row(w0), w1, w2, _row(a0), a1, a2, g1, g2, _row(k_k), _row(k_a))
    r, k, v, lw, kk, bvec, gate = outs

    tblk = min(tblk, s)
    width = SCAN_PAIRS * LANES
    blk = pl.BlockSpec((1, tblk, width), lambda bi, hp, i: (bi, i, hp))
    chan = pl.BlockSpec((1, width), lambda bi, hp, i: (0, hp))
    y = pl.pallas_call(
        functools.partial(_rwkv_scan_kernel, chunk=RWKV_CHUNK),
        grid=(bsz, d // width, s // tblk),
        in_specs=[blk] * 6 + [chan] * 3,
        out_specs=blk,
        out_shape=jax.ShapeDtypeStruct((bsz, s, d), F32),
        scratch_shapes=[pltpu.VMEM((SCAN_PAIRS, LANES, LANES), F32)],
        compiler_params=_cparams("parallel", "parallel", "arbitrary"),
        name="rwkv_scan",
    )(r, k, v, lw, kk, bvec, _row(lnx_g), _row(lnx_b), _row(r_k))

    n = bsz * s
    tm2 = min(tm, n)
    row = pl.BlockSpec((tm2, d), lambda i: (i, 0))
    flat = lambda z: z.reshape(n, d)
    out = pl.pallas_call(
        functools.partial(_rwkv_post_kernel, alpha=alpha),
        grid=(n // tm2,),
        in_specs=[row] * 3 + [_layer(w_out, layer), _full((1, d)), _full((1, d))],
        out_specs=row,
        out_shape=jax.ShapeDtypeStruct((n, d), F32),
        compiler_params=_cparams("parallel"),
        name="rwkv_post",
    )(flat(y), flat(gate), flat(x), w_out, _row(g), _row(b))
    return out.reshape(bsz, s, d)


def kernel(x, mem, rel_bias, a_w_in, a_b_in, a_dw, a_dw_b, a_ln_g, a_ln_b, a_w_out, a_b_out, b_w_qkv, b_w_out, c_mu, c_w_rkv, c_w0, c_w1, c_w2, c_a0, c_a1, c_a2, c_g1, c_g2, c_k_k, c_k_a, c_r_k, c_lnx_g, c_lnx_b, c_w_out, x_w_q, x_w_kv, x_w_out, m_w1, m_w2, ln_g, ln_b):
    depth = ln_g.shape[0]
    alpha = (2 * depth) ** 0.25
    bsz, s, d = x.shape
    mlen = mem.shape[1]
    mem2d = mem.reshape(bsz * mlen, d)
    for i in range(depth):
        kind, j = i % 3, i // 3
        if kind == 0:
            x = _conv_mixer(x, a_w_in, j, a_b_in[j], a_dw[j], a_dw_b[j], a_ln_g[j], a_ln_b[j],
                            a_w_out, a_b_out[j], ln_g[i, 0], ln_b[i, 0], alpha)
        elif kind == 1:
            x = _dil_mixer(x, b_w_qkv, b_w_out, j, rel_bias, ln_g[i, 0], ln_b[i, 0], alpha)
        else:
            x = _rwkv_mixer(x, j, c_mu[j], c_w_rkv, c_w0[j], c_w1, c_w2, c_a0[j],
                            c_a1, c_a2, c_g1, c_g2, c_k_k[j], c_k_a[j],
                            c_r_k[j], c_lnx_g[j], c_lnx_b[j], c_w_out,
                            ln_g[i, 0], ln_b[i, 0], alpha)
        kv = _linear(mem2d, x_w_kv, i, MXU_DTYPE, 512, 1024).reshape(bsz, mlen, 2 * d)
        x = _cross_attention(x, kv, x_w_q, x_w_out, i, ln_g[i, 1], ln_b[i, 1], alpha)
        x = _mlp(x.reshape(bsz * s, d), m_w1, m_w2, i, ln_g[i, 2], ln_b[i, 2],
                 alpha).reshape(bsz, s, d)
    return x
```

```python
import functools
import math

import jax
import jax.numpy as jnp
from jax import lax
from jax.experimental import pallas as pl
from jax.experimental.pallas import tpu as pltpu

F32 = jnp.float32
BF16 = jnp.bfloat16
MXU_DTYPE = jnp.bfloat16

CONV_WIDTH = 31
DIL_GROUPS = ((128, 1), (512, 4), (2048, 16))
DIL_HEADS = 16
DIL_HEAD_DIM = 64
REL_BUCKETS = 32
REL_MAX_DIST = 2048
RWKV_HEAD = 64
RWKV_GN_EPS = 64e-5
XATTN_HEADS = 4
LN_EPS = 1e-5
NEG_BIG = -1e30

LANES = 128
SUBLANES = 8
VMEM_LIMIT_BYTES = 56 * 1024 * 1024

RWKV_CHUNK = 64
TRI_BASE = 8
SCAN_PAIRS = 4


def _cparams(*sem):
    return pltpu.CompilerParams(dimension_semantics=sem, vmem_limit_bytes=VMEM_LIMIT_BYTES)


def _mm(a, b):
    return jnp.dot(a.astype(MXU_DTYPE), b.astype(MXU_DTYPE), preferred_element_type=F32)


def _mm_nt(a, b):
    return lax.dot_general(a.astype(MXU_DTYPE), b.astype(MXU_DTYPE),
                           (((1,), (1,)), ((), ())), preferred_element_type=F32)


def _split3(x):
    hi = x.astype(BF16)
    r1 = x - hi.astype(F32)
    mid = r1.astype(BF16)
    lo = (r1 - mid.astype(F32)).astype(BF16)
    return hi, mid, lo


def _mm_exact_rhs(x, e, passes=3):
    pieces = _split3(x)[:passes]
    return sum(jnp.dot(p, e, preferred_element_type=F32) for p in pieces)


def _mm_exact_lhs(e, x):
    hi, mid, lo = _split3(x)
    dot = functools.partial(jnp.dot, preferred_element_type=F32)
    return dot(e, hi) + dot(e, mid) + dot(e, lo)


def _sigmoid(z):
    return 1.0 / (1.0 + jnp.exp(-z))


def _softplus(z):
    return jnp.maximum(z, 0.0) + jnp.log(1.0 + jnp.exp(-jnp.abs(z)))


def _layer_norm(z, g, b, eps=LN_EPS):
    mu = jnp.mean(z, axis=-1, keepdims=True)
    zc = z - mu
    var = jnp.mean(zc * zc, axis=-1, keepdims=True)
    return zc * lax.rsqrt(var + eps) * g + b


def _postnorm(alpha, x, h, g, b):
    return _layer_norm(alpha * x + h, g, b)


def _full(shape):
    nd = len(shape)
    return pl.BlockSpec(shape, lambda *_: (0,) * nd)


def _resident(shape):
    nd = len(shape)
    return pl.BlockSpec(shape, lambda *_: (0,) * nd, pipeline_mode=pl.Buffered(1))


def _layer(arr, layer):
    nd = arr.ndim - 1
    return pl.BlockSpec((None,) + arr.shape[1:], lambda *_: (layer,) + (0,) * nd,
                        pipeline_mode=pl.Buffered(1))


def _row(v):
    return v.reshape(1, -1)


def _linear_kernel(x_ref, w_ref, o_ref):
    o_ref[...] = _mm(x_ref[...], w_ref[...]).astype(o_ref.dtype)


def _linear(x2d, w, layer, out_dtype, tm, tn):
    m, k = x2d.shape
    n = w.shape[2]
    tm, tn = min(tm, m), min(tn, n)
    return pl.pallas_call(
        _linear_kernel,
        grid=(m // tm, n // tn),
        in_specs=[pl.BlockSpec((tm, k), lambda i, j: (i, 0)),
                  pl.BlockSpec((None, k, tn), lambda i, j: (layer, 0, j))],
        out_specs=pl.BlockSpec((tm, tn), lambda i, j: (i, j)),
        out_shape=jax.ShapeDtypeStruct((m, n), out_dtype),
        compiler_params=_cparams("parallel", "parallel"),
        name="linear",
    )(x2d, w)


def _xattn_kernel(x_ref, kv_ref, wq_ref, wo_ref, g_ref, b_ref, o_ref, oh_ref, *, heads, alpha):
    xb = x_ref[0]
    d = xb.shape[-1]
    e = d // heads
    q = (_mm(xb, wq_ref[...]) * (e ** -0.5)).astype(MXU_DTYPE)
    for h in range(heads):
        kh = kv_ref[0, :, h * e:(h + 1) * e]
        vh = kv_ref[0, :, d + h * e:d + (h + 1) * e]
        s = _mm_nt(q[:, h * e:(h + 1) * e], kh)
        m = jnp.max(s, axis=-1, keepdims=True)
        p = jnp.exp(s - m)
        den = jnp.sum(p, axis=-1, keepdims=True)
        oh_ref[:, h * e:(h + 1) * e] = _mm(p, vh) / den
    hout = _mm(oh_ref[...], wo_ref[...])
    o_ref[0] = _postnorm(alpha, xb, hout, g_ref[...], b_ref[...])


def _cross_attention(x, kv, wq, wo, layer, g, b, alpha, tm=1024):
    bsz, s, d = x.shape
    mlen = kv.shape[1]
    tm = min(tm, s)
    return pl.pallas_call(
        functools.partial(_xattn_kernel, heads=XATTN_HEADS, alpha=alpha),
        grid=(bsz, s // tm),
        in_specs=[pl.BlockSpec((1, tm, d), lambda bi, i: (bi, i, 0)),
                  pl.BlockSpec((1, mlen, 2 * d), lambda bi, i: (bi, 0, 0)),
                  _layer(wq, layer), _layer(wo, layer), _full((1, d)), _full((1, d))],
        out_specs=pl.BlockSpec((1, tm, d), lambda bi, i: (bi, i, 0)),
        out_shape=jax.ShapeDtypeStruct((bsz, s, d), F32),
        scratch_shapes=[pltpu.VMEM((tm, d), F32)],
        compiler_params=_cparams("parallel", "parallel"),
        name="cross_attention",
    )(x, kv, wq, wo, _row(g), _row(b))


def _mlp_kernel(x_ref, w1_ref, w2_ref, g_ref, b_ref, o_ref, *, fchunk, alpha):
    xb = x_ref[...]
    xm = xb.astype(MXU_DTYPE)
    ff = w1_ref.shape[1]
    acc = jnp.zeros(xb.shape, F32)
    for c in range(ff // fchunk):
        hid = _mm(xm, w1_ref[:, c * fchunk:(c + 1) * fchunk])
        hid = jnp.maximum(hid, 0.0)
        acc = acc + _mm(hid * hid, w2_ref[c * fchunk:(c + 1) * fchunk, :])
    o_ref[...] = _postnorm(alpha, xb, acc, g_ref[...], b_ref[...])


def _mlp(x2d, w1, w2, layer, g, b, alpha, tm=512, fchunk=1024):
    n, d = x2d.shape
    ff = w1.shape[2]
    tm = min(tm, n)
    return pl.pallas_call(
        functools.partial(_mlp_kernel, fchunk=min(fchunk, ff), alpha=alpha),
        grid=(n // tm,),
        in_specs=[pl.BlockSpec((tm, d), lambda i: (i, 0)),
                  _layer(w1, layer), _layer(w2, layer), _full((1, d)), _full((1, d))],
        out_specs=pl.BlockSpec((tm, d), lambda i: (i, 0)),
        out_shape=jax.ShapeDtypeStruct((n, d), F32),
        compiler_params=_cparams("parallel"),
        name="sq_relu_mlp",
    )(x2d, w1, w2, _row(g), _row(b))


CONV_HALO = 32
CONV_PARTS = 2


def _conv_kernel(x_ref, win_ref, bin_ref, dw_ref, dwb_ref, lng_ref, lnb_ref, wo_ref, bo_ref,
                 g_ref, b_ref, o_ref, cat_ref, sh_ref, *, alpha):
    i = pl.program_id(1)
    tm, d = x_ref.shape[1], x_ref.shape[2]

    @pl.when(i == 0)
    def _():
        cat_ref[0:CONV_HALO, :] = jnp.zeros((CONV_HALO, d), F32)

    @pl.when(i > 0)
    def _():
        cat_ref[0:CONV_HALO, :] = cat_ref[tm:tm + CONV_HALO, :]

    xb = x_ref[0]
    parts = sh_ref.shape[0]
    pm = tm // parts
    for part in range(parts):
        r0 = part * pm
        h = _mm(xb[r0:r0 + pm], win_ref[...]) + bin_ref[...]
        cat_ref[CONV_HALO + r0:CONV_HALO + r0 + pm, :] = h[:, :d] * _sigmoid(h[:, d:])
    off = CONV_HALO - (CONV_WIDTH - 1)
    for part in range(parts):
        r0 = part * pm
        acc = jnp.broadcast_to(dwb_ref[...], (pm, d))
        for phase in range(SUBLANES):
            taps = [j for j in range(CONV_WIDTH) if (off + j) % SUBLANES == phase]
            span = max(off + j for j in taps) - phase + pm
            if phase:
                sh_ref[part, 0:span, :] = cat_ref[r0 + phase:r0 + phase + span, :]
            for j in taps:
                a = off + j - phase
                rows = sh_ref[part, a:a + pm, :] if phase else cat_ref[r0 + a:r0 + a + pm, :]
                acc = acc + dw_ref[j:j + 1, :] * rows
        z = _layer_norm(acc, lng_ref[...], lnb_ref[...])
        z = z * _sigmoid(z)
        hout = _mm(z, wo_ref[...]) + bo_ref[...]
        o_ref[0, r0:r0 + pm, :] = _postnorm(alpha, xb[r0:r0 + pm], hout, g_ref[...], b_ref[...])


def _conv_mixer(x, w_in, layer, b_in, dw, dw_b, ln_g, ln_b, w_out, b_out, g, b, alpha, tm=512):
    bsz, s, d = x.shape
    tm = min(tm, s)
    pm = tm // CONV_PARTS
    return pl.pallas_call(
        functools.partial(_conv_kernel, alpha=alpha),
        grid=(bsz, s // tm),
        in_specs=[pl.BlockSpec((1, tm, d), lambda bi, i: (bi, i, 0)),
                  _layer(w_in, layer), _full((1, 2 * d)),
                  _full((CONV_WIDTH, d)), _full((1, d)), _full((1, d)), _full((1, d)),
                  _layer(w_out, layer), _full((1, d)), _full((1, d)), _full((1, d))],
        out_specs=pl.BlockSpec((1, tm, d), lambda bi, i: (bi, i, 0)),
        out_shape=jax.ShapeDtypeStruct((bsz, s, d), F32),
        scratch_shapes=[pltpu.VMEM((CONV_HALO + tm, d), F32),
                        pltpu.VMEM((CONV_PARTS, CONV_HALO + pm, d), F32)],
        compiler_params=_cparams("arbitrary", "arbitrary"),
        name="conv_module",
    )(x, w_in, _row(b_in), dw, _row(dw_b), _row(ln_g), _row(ln_b), w_out, _row(b_out),
      _row(g), _row(b))


def _t5_causal_bucket(dist):
    n = jnp.maximum(dist, 0)
    max_exact = REL_BUCKETS // 2
    nf = jnp.maximum(n, 1).astype(F32)
    large = max_exact + (jnp.log(nf / max_exact) / math.log(REL_MAX_DIST / max_exact)
                         * (REL_BUCKETS - max_exact)).astype(jnp.int32)
    large = jnp.minimum(large, REL_BUCKETS - 1)
    return jnp.where(n < max_exact, n, large)


def _qkv_kernel(x_ref, w_ref, o_ref, xs_ref, xp_ref, *, dil, qscale):
    tm = x_ref.shape[1]
    n = tm // dil
    if dil == 1:
        xp = x_ref[0]
    else:
        for j in range(xs_ref.shape[0]):
            cols = slice(j * LANES, (j + 1) * LANES)
            xs_ref[j] = x_ref[0, :, cols]
            for c in range(dil):
                xp_ref[c * n:(c + 1) * n, cols] = (
                    xs_ref[j, pl.ds(c, n, stride=dil), :].astype(xp_ref.dtype))
        xp = xp_ref[...]
    acc = _mm(xp, w_ref[...])
    nblk = o_ref.shape[2]
    for c in range(dil):
        for j in range(nblk):
            tile = acc[c * n:(c + 1) * n, j * LANES:(j + 1) * LANES]
            if 3 * j < nblk:
                tile = tile * qscale
            o_ref[0, c, j] = tile.astype(o_ref.dtype)


def _qkv_proj(x, w_qkv, layer, group, dil, tm=512):
    bsz, s, d = x.shape
    sub = s // dil
    ncol = w_qkv.shape[2] // len(DIL_GROUPS)
    nblk = ncol // LANES
    tm = min(tm, s)
    return pl.pallas_call(
        functools.partial(_qkv_kernel, dil=dil, qscale=DIL_HEAD_DIM ** -0.5),
        grid=(bsz, s // tm),
        in_specs=[pl.BlockSpec((1, tm, d), lambda bi, i: (bi, i, 0)),
                  pl.BlockSpec((None, d, ncol), lambda bi, i: (layer, 0, group),
                               pipeline_mode=pl.Buffered(1))],
        out_specs=pl.BlockSpec((1, dil, nblk, tm // dil, LANES), lambda bi, i: (bi, 0, 0, i, 0)),
        out_shape=jax.ShapeDtypeStruct((bsz, dil, nblk, sub, LANES), MXU_DTYPE),
        scratch_shapes=[pltpu.VMEM((d // LANES, tm, LANES), F32), pltpu.VMEM((tm, d), MXU_DTYPE)],
        compiler_params=_cparams("parallel", "parallel"),
        name="dil_qkv_proj",
    )(x, w_qkv)


def _dil_bias_kernel(bucket_ref, relb_ref, bias_ref, *, heads, blk):
    bkt = bucket_ref[...]
    kcol = lax.broadcasted_iota(jnp.int32, bkt.shape, 1)
    for h in range(heads):
        acc = jnp.full(bkt.shape, NEG_BIG, F32)
        for n in range(REL_BUCKETS):
            acc = jnp.where(bkt == n, relb_ref[n, h], acc)
        bias_ref[h] = acc
        bias_ref[heads + h] = jnp.where(kcol < blk, NEG_BIG, acc)


def _dil_attn_kernel(bias_ref, q_ref, k_ref, v_ref, kp_ref, vp_ref, o_ref, l_ref,
                     kc_ref, vc_ref, *, heads, blk):
    i = pl.program_id(2)
    tq = q_ref.shape[3]
    kc_ref[:, 0:blk, :] = kp_ref[0, 0]
    kc_ref[:, blk:blk + tq, :] = k_ref[0, 0]
    vc_ref[:, 0:blk, :] = vp_ref[0, 0]
    vc_ref[:, blk:blk + tq, :] = v_ref[0, 0]
    lane = lax.broadcasted_iota(jnp.int32, (blk, LANES), 1)
    low = lane < DIL_HEAD_DIM

    for jb in range(tq // blk):
        rows = slice(jb * blk, (jb + 1) * blk)
        keys = slice(jb * blk, (jb + 2) * blk)
        table = jnp.where(i == 0, heads, 0) if jb == 0 else 0

        def pair(hp, lse_all):
            q2 = q_ref[0, 0, hp, rows, :]
            k2 = kc_ref[hp, keys, :]
            v2 = vc_ref[hp, keys, :]
            outs = []
            for hh in range(2):
                h = 2 * hp + hh
                keep = low if hh == 0 else jnp.logical_not(low)
                qm = jnp.where(keep, q2, jnp.zeros_like(q2))
                s = _mm_nt(qm, k2) + bias_ref[table + h]
                m = jnp.max(s, axis=-1, keepdims=True)
                p = jnp.exp(s - m)
                den = jnp.sum(p, axis=-1, keepdims=True)
                outs.append(_mm(p, v2) / den)
                lse_all = jnp.where(lane == h, m + jnp.log(den), lse_all)
            o_ref[0, 0, hp, rows, :] = jnp.where(low, outs[0], outs[1]).astype(o_ref.dtype)
            return lse_all

        lse_all = jnp.zeros((blk, LANES), F32)
        for hp in range(heads // 2):
            lse_all = pair(hp, lse_all)
        l_ref[0, 0, rows, :] = lse_all


def _dil_attention_group(qkv, rel_bias, window, dil, tq=512):
    bsz, _, nblk, sub, _ = qkv.shape
    hb = nblk // 3
    blk = window // dil
    tq = min(tq, sub)
    nsub = tq // blk
    qi = jnp.arange(blk)[:, None]
    kj = jnp.arange(2 * blk)[None, :]
    rel = qi + blk - kj
    bucket = jnp.where((rel >= 0) & (rel <= blk), _t5_causal_bucket(rel * dil), -1).astype(jnp.int32)

    bias = pl.pallas_call(
        functools.partial(_dil_bias_kernel, heads=DIL_HEADS, blk=blk),
        in_specs=[pl.BlockSpec(memory_space=pltpu.VMEM), pl.BlockSpec(memory_space=pltpu.SMEM)],
        out_specs=pl.BlockSpec(memory_space=pltpu.VMEM),
        out_shape=jax.ShapeDtypeStruct((2 * DIL_HEADS, blk, 2 * blk), F32),
        name="dil_bias_table",
    )(bucket, rel_bias)

    def cur(which):
        return pl.BlockSpec((1, 1, hb, tq, LANES), lambda bi, c, i: (bi, c, which, i, 0))

    def prev(which):
        return pl.BlockSpec((1, 1, hb, blk, LANES),
                            lambda bi, c, i: (bi, c, which, jnp.maximum(i * nsub - 1, 0), 0))

    return pl.pallas_call(
        functools.partial(_dil_attn_kernel, heads=DIL_HEADS, blk=blk),
        grid=(bsz, dil, sub // tq),
        in_specs=[_resident((2 * DIL_HEADS, blk, 2 * blk)), cur(0), cur(1), cur(2), prev(1), prev(2)],
        out_specs=[pl.BlockSpec((1, 1, hb, tq, LANES), lambda bi, c, i: (bi, c, 0, i, 0)),
                   pl.BlockSpec((1, 1, tq, LANES), lambda bi, c, i: (bi, c, i, 0))],
        out_shape=[jax.ShapeDtypeStruct((bsz, dil, hb, sub, LANES), MXU_DTYPE),
                   jax.ShapeDtypeStruct((bsz, dil, sub, LANES), F32)],
        scratch_shapes=[pltpu.VMEM((hb, blk + tq, LANES), MXU_DTYPE),
                        pltpu.VMEM((hb, blk + tq, LANES), MXU_DTYPE)],
        compiler_params=_cparams("parallel", "parallel", "parallel"),
        name="dil_attention",
    )(bias, qkv, qkv, qkv, qkv, qkv)


def _dil_out_kernel(o1_ref, o2_ref, o3_ref, l1_ref, l2_ref, l3_ref, x_ref, e_ref, wo_ref, g_ref,
                    b_ref, out_ref, os_ref, ls_ref, *, alpha, dils):
    tm = x_ref.shape[1]
    o_refs, l_refs = (o1_ref, o2_ref, o3_ref), (l1_ref, l2_ref, l3_ref)

    def natural(dst_ref, src, dil):
        if dil == 1:
            return src(0).astype(F32)
        n = tm // dil
        for c in range(dil):
            dst_ref[pl.ds(c, n, stride=dil), :] = src(c).astype(F32)
        return dst_ref[...]

    l1, l2, l3 = [natural(ls_ref.at[g], lambda c, r=l_refs[g]: r[0, c], dils[g]) for g in range(3)]
    m = jnp.maximum(jnp.maximum(l1, l2), l3)
    es = [jnp.exp(l1 - m), jnp.exp(l2 - m), jnp.exp(l3 - m)]
    inv = 1.0 / (es[0] + es[1] + es[2])
    ws = [_mm_exact_rhs(e * inv, e_ref[...], passes=2) for e in es]
    cols = []
    for j in range(o1_ref.shape[2]):
        blk = slice(j * LANES, (j + 1) * LANES)
        cols.append(sum(ws[g][:, blk] * natural(os_ref.at[g, j],
                                                lambda c, r=o_refs[g], j=j: r[0, c, j], dils[g])
                        for g in range(3)))
    hout = _mm(jnp.concatenate(cols, axis=1), wo_ref[...])
    out_ref[0] = _postnorm(alpha, x_ref[0], hout, g_ref[...], b_ref[...])


def _dil_mixer(x, w_qkv, w_out, layer, rel_bias, g, b, alpha, tm=256):
    bsz, s, d = x.shape
    he = DIL_HEADS * DIL_HEAD_DIM
    hb = he // LANES
    dils = tuple(dil for _, dil in DIL_GROUPS)
    outs, lses = [], []
    for gi, (window, dil) in enumerate(DIL_GROUPS):
        qkv = _qkv_proj(x, w_qkv, layer, gi, dil)
        o, lse = _dil_attention_group(qkv, rel_bias, window, dil)
        outs.append(o)
        lses.append(lse)
    tm = min(tm, s)
    o_specs = [pl.BlockSpec((1, dil, hb, tm // dil, LANES), lambda bi, i: (bi, 0, 0, i, 0))
               for dil in dils]
    l_specs = [pl.BlockSpec((1, dil, tm // dil, LANES), lambda bi, i: (bi, 0, i, 0)) for dil in dils]
    tile = pl.BlockSpec((1, tm, d), lambda bi, i: (bi, i, 0))
    expand = (jnp.arange(LANES)[:, None] == jnp.arange(he)[None, :] // DIL_HEAD_DIM).astype(BF16)
    return pl.pallas_call(
        functools.partial(_dil_out_kernel, alpha=alpha, dils=dils),
        grid=(bsz, s // tm),
        in_specs=o_specs + l_specs + [tile, _full((LANES, he)), _layer(w_out, layer),
                                      _full((1, d)), _full((1, d))],
        out_specs=tile,
        out_shape=jax.ShapeDtypeStruct((bsz, s, d), F32),
        scratch_shapes=[pltpu.VMEM((len(dils), hb, tm, LANES), F32),
                        pltpu.VMEM((len(dils), tm, LANES), F32)],
        compiler_params=_cparams("parallel", "parallel"),
        name="dil_combine_out",
    )(*outs, *lses, x, expand, w_out, _row(g), _row(b))


def _rwkv_prep_kernel(x_ref, xh_ref, mu_ref, wrkv_ref, w0_ref, w1_ref, w2_ref, a0_ref, a1_ref,
                      a2_ref, g1_ref, g2_ref, kk_ref, ka_ref,
                      r_o, k_o, v_o, lw_o, kk_o, b_o, g_o):
    i = pl.program_id(1)
    x = x_ref[0]
    tm = x.shape[0]
    prev_row = jnp.where(i == 0, 0.0, xh_ref[0, SUBLANES - 1:SUBLANES, :])
    rows = lax.broadcasted_iota(jnp.int32, (tm, 1), 0)
    xprev = jnp.where(rows == 0, prev_row, pltpu.roll(x, 1, 0))
    xx = xprev - x

    def mix(j):
        return x + xx * mu_ref[j:j + 1, :]

    r = _mm(mix(0), wrkv_ref[0])
    k = _mm(mix(1), wrkv_ref[1])
    v = _mm(mix(2), wrkv_ref[2])
    wl = w0_ref[...] + _mm(jnp.tanh(_mm(mix(3), w1_ref[...])), w2_ref[...])
    w_log = -_softplus(-wl) - 0.5
    a = _sigmoid(a0_ref[...] + _mm(_mm(mix(4), a1_ref[...]), a2_ref[...]))
    g = _mm(_sigmoid(_mm(mix(5), g1_ref[...])), g2_ref[...])
    kk = k * kk_ref[...]
    r_o[0] = r
    k_o[0] = k * (1.0 + (a - 1.0) * ka_ref[...])
    v_o[0] = v
    lw_o[0] = -jnp.exp(w_log)
    kk_o[0] = kk
    b_o[0] = kk * a
    g_o[0] = g


def _tri_inv(mats, blk):
    n = mats[0].shape[0]
    r = lax.broadcasted_iota(jnp.int32, (n, n), 0)
    c = lax.broadcasted_iota(jnp.int32, (n, n), 1)

    def same(bs):
        sh = int(math.log2(bs))
        return lax.shift_right_logical(r, sh) == lax.shift_right_logical(c, sh)

    base = same(TRI_BASE)
    eye = jnp.where(r == c, 1.0, 0.0)
    ps = [jnp.where(base, a, 0.0) for a in mats]
    ts = [eye + p for p in ps]
    ps = [_mm(p, p) for p in ps]
    m = 2
    while 2 * m < TRI_BASE:
        both = [_mm(p, jnp.concatenate([p, t], axis=1)) for p, t in zip(ps, ts)]
        ts = [t + x[:, n:] for t, x in zip(ts, both)]
        ps = [x[:, :n] for x in both]
        m *= 2
    ts = [t + _mm(p, t) for p, t in zip(ps, ts)]
    bs = TRI_BASE
    while bs < blk:
        sel = same(2 * bs) & jnp.logical_not(same(bs))
        nblk = n // bs
        rows = lambda z, i: z[i * bs:(i + 1) * bs]
        t_odd = [jnp.concatenate([rows(t, i) for i in range(1, nblk, 2)], axis=0) for t in ts]
        lo = [_mm(to, jnp.where(sel, a, 0.0)) for to, a in zip(t_odd, mats)]
        upd = [_mm(l, t) for l, t in zip(lo, ts)]
        ts = [jnp.concatenate([rows(t, i) + rows(u, i // 2) if i % 2 else rows(t, i)
                               for i in range(nblk)], axis=0)
              for t, u in zip(ts, upd)]
        bs *= 2
    return ts


def _rwkv_scan_kernel(r_ref, k_ref, v_ref, lw_ref, kk_ref, b_ref, lg_ref, lb_ref, rk_ref,
                      y_ref, st_ref, *, chunk):
    @pl.when(pl.program_id(2) == 0)
    def _():
        st_ref[...] = jnp.zeros_like(st_ref)

    L = chunk
    n2 = 2 * L
    tb, width = r_ref.shape[1], r_ref.shape[2]
    chunks = range(tb // L)
    pairs = range(width // LANES)
    probs = [(p, ch) for ch in chunks for p in pairs]
    sh = int(math.log2(L))
    lane = lax.broadcasted_iota(jnp.int32, (L, LANES), 1)
    head0 = lane < RWKV_HEAD
    rr = lax.broadcasted_iota(jnp.int32, (n2, n2), 0)
    cc = lax.broadcasted_iota(jnp.int32, (n2, n2), 1)
    rt, ct = rr & (L - 1), cc & (L - 1)
    strict, incl, eye = rt > ct, rt >= ct, rr == cc
    tr = lax.broadcasted_iota(jnp.int32, (tb, tb), 0)
    tc = lax.broadcasted_iota(jnp.int32, (tb, tb), 1)
    same_chunk = lax.shift_right_logical(tr, sh) == lax.shift_right_logical(tc, sh)
    head0_tb = lax.broadcasted_iota(jnp.int32, (tb, LANES), 1) < RWKV_HEAD

    def head_sum(z):
        out = []
        for p in pairs:
            zp = z[:, p * LANES:(p + 1) * LANES]
            s0 = jnp.sum(jnp.where(head0_tb, zp, 0.0), axis=-1, keepdims=True)
            s1 = jnp.sum(jnp.where(head0_tb, 0.0, zp), axis=-1, keepdims=True)
            out.append(jnp.where(head0_tb, s0, s1))
        return jnp.concatenate(out, axis=1)

    r, k, v = r_ref[0], k_ref[0], v_ref[0]
    lw, kk = lw_ref[0], kk_ref[0]
    inv_norm = 1.0 / jnp.maximum(jnp.sqrt(head_sum(kk * kk)), 1e-12)
    kn = kk * inv_norm
    b = b_ref[0] * inv_norm
    tri = jnp.where(same_chunk & (tr >= tc), 1.0, 0.0).astype(BF16)
    c = _mm_exact_lhs(tri, lw)
    c_end = jnp.concatenate(
        [jnp.broadcast_to(c[(ch + 1) * L - 1:(ch + 1) * L, :], (L, width)) for ch in chunks], axis=0)
    e_inv = jnp.exp(-c)
    e_end = jnp.exp(c_end - c)
    p_end = jnp.exp(c_end)
    a_t = -kn * jnp.exp(c - lw)
    r_t = r * jnp.exp(c)
    bh, kh = b * e_inv, k * e_inv
    bb, kb = b * e_end, k * e_end

    def stack(z, prob):
        p, ch = prob
        z = z[ch * L:(ch + 1) * L, p * LANES:(p + 1) * LANES]
        return jnp.concatenate([jnp.where(head0, z, 0.0), jnp.where(head0, 0.0, z)], axis=0)

    a_s = [stack(a_t, q) for q in probs]
    r_s = [stack(r_t, q) for q in probs]
    v_s = [stack(v, q) for q in probs]
    ms = [_mm_nt(jnp.concatenate([a, rs], axis=0),
                 jnp.concatenate([stack(bh, q), stack(kh, q)], axis=0))
          for a, rs, q in zip(a_s, r_s, probs)]
    ts = _tri_inv([jnp.where(strict, m[:n2, :n2], 0.0) for m in ms], L)
    av = [_mm(jnp.where(strict, m[:n2, n2:], 0.0), vs) for m, vs in zip(ms, v_s)]
    tu = [_mm(t, jnp.concatenate([a, x], axis=1)) for t, a, x in zip(ts, a_s, av)]
    zeros = jnp.zeros((n2, LANES), F32)
    incl2 = jnp.concatenate([incl, incl], axis=1)
    big = [_mm(jnp.concatenate([jnp.where(incl2, m[n2:], 0.0),
                                jnp.concatenate([stack(bb, q), stack(kb, q)], axis=0).T], axis=0),
               jnp.concatenate([x, jnp.concatenate([zeros, vs], axis=1)], axis=0))
           for m, q, x, vs in zip(ms, probs, tu, v_s)]
    rg_lhs, y0, h0 = {}, {}, {}
    for i, (p, ch) in enumerate(probs):
        p_last = p_end[ch * L:ch * L + 1, p * LANES:(p + 1) * LANES]
        rg_lhs[p, ch] = big[i][:, :LANES] + jnp.concatenate(
            [r_s[i], jnp.where(eye, p_last, 0.0)], axis=0)
        y0[p, ch] = big[i][:n2, LANES:]
        h0[p, ch] = big[i][n2:, LANES:]

    st = [st_ref[p] for p in pairs]
    ys = [[] for _ in pairs]
    for ch in chunks:
        for p in pairs:
            rg = _mm(rg_lhs[p, ch], st[p])
            y2 = rg[:n2] + y0[p, ch]
            st[p] = rg[n2:] + h0[p, ch]
            ys[p].append(y2[:L] + y2[L:])
    for p in pairs:
        st_ref[p] = st[p]

    y = jnp.concatenate([jnp.concatenate(yp, axis=0) for yp in ys], axis=1)
    inv_n = 1.0 / RWKV_HEAD
    yc = y - head_sum(y) * inv_n
    var = head_sum(yc * yc) * inv_n
    yn = yc * lax.rsqrt(var + RWKV_GN_EPS) * lg_ref[...] + lb_ref[...]
    y_ref[0] = yn + head_sum(r * k * rk_ref[...]) * v


def _rwkv_post_kernel(y_ref, gate_ref, x_ref, wo_ref, g_ref, b_ref, o_ref, *, alpha):
    hout = _mm(y_ref[...] * gate_ref[...], wo_ref[...])
    o_ref[...] = _postnorm(alpha, x_ref[...], hout, g_ref[...], b_ref[...])


def _rwkv_mixer(x, layer, mu, w_rkv, w0, w1, w2, a0, a1, a2, g1, g2, k_k, k_a, r_k, lnx_g, lnx_b,
                w_out, g, b, alpha, tm=512, tblk=256):
    bsz, s, d = x.shape
    tm = min(tm, s)
    tile = pl.BlockSpec((1, tm, d), lambda bi, i: (bi, i, 0))
    hb = tm // SUBLANES
    outs = pl.pallas_call(
        _rwkv_prep_kernel,
        grid=(bsz, s // tm),
        in_specs=[tile,
                  pl.BlockSpec((1, SUBLANES, d), lambda bi, i: (bi, jnp.maximum(i * hb - 1, 0), 0)),
                  _full((6, d)), _layer(w_rkv, layer), _full((1, d)),
                  _layer(w1, layer), _layer(w2, layer), _full((1, d)),
                  _layer(a1, layer), _layer(a2, layer),
                  _layer(g1, layer), _layer(g2, layer),
                  _full((1, d)), _full((1, d))],
        out_specs=[tile] * 7,
        out_shape=[jax.ShapeDtypeStruct((bsz, s, d), F32)] * 7,
        compiler_params=_cparams("parallel", "parallel"),
        name="rwkv_prep",
    )(x, x, mu, w_rkv, _row(w0), w1, w2, _row(a0), a1, a2, g1, g2, _row(k_k), _row(k_a))
    r, k, v, lw, kk, bvec, gate = outs

    tblk = min(tblk, s)
    width = SCAN_PAIRS * LANES
    blk = pl.BlockSpec((1, tblk, width), lambda bi, hp, i: (bi, i, hp))
    chan = pl.BlockSpec((1, width), lambda bi, hp, i: (0, hp))
    y = pl.pallas_call(
        functools.partial(_rwkv_scan_kernel, chunk=RWKV_CHUNK),
        grid=(bsz, d // width, s // tblk),
        in_specs=[blk] * 6 + [chan] * 3,
        out_specs=blk,
        out_shape=jax.ShapeDtypeStruct((bsz, s, d), F32),
        scratch_shapes=[pltpu.VMEM((SCAN_PAIRS, LANES, LANES), F32)],
        compiler_params=_cparams("parallel", "parallel", "arbitrary"),
        name="rwkv_scan",
    )(r, k, v, lw, kk, bvec, _row(lnx_g), _row(lnx_b), _row(r_k))

    n = bsz * s
    tm2 = min(tm, n)
    row = pl.BlockSpec((tm2, d), lambda i: (i, 0))
    flat = lambda z: z.reshape(n, d)
    out = pl.pallas_call(
        functools.partial(_rwkv_post_kernel, alpha=alpha),
        grid=(n // tm2,),
        in_specs=[row] * 3 + [_layer(w_out, layer), _full((1, d)), _full((1, d))],
        out_specs=row,
        out_shape=jax.ShapeDtypeStruct((n, d), F32),
        compiler_params=_cparams("parallel"),
        name="rwkv_post",
    )(flat(y), flat(gate), flat(x), w_out, _row(g), _row(b))
    return out.reshape(bsz, s, d)


def kernel(x, mem, rel_bias, a_w_in, a_b_in, a_dw, a_dw_b, a_ln_g, a_ln_b, a_w_out, a_b_out, b_w_qkv, b_w_out, c_mu, c_w_rkv, c_w0, c_w1, c_w2, c_a0, c_a1, c_a2, c_g1, c_g2, c_k_k, c_k_a, c_r_k, c_lnx_g, c_lnx_b, c_w_out, x_w_q, x_w_kv, x_w_out, m_w1, m_w2, ln_g, ln_b):
    depth = ln_g.shape[0]
    alpha = (2 * depth) ** 0.25
    bsz, s, d = x.shape
    mlen = mem.shape[1]
    mem2d = mem.reshape(bsz * mlen, d)
    for i in range(depth):
        kind, j = i % 3, i // 3
        if kind == 0:
            x = _conv_mixer(x, a_w_in, j, a_b_in[j], a_dw[j], a_dw_b[j], a_ln_g[j], a_ln_b[j],
                            a_w_out, a_b_out[j], ln_g[i, 0], ln_b[i, 0], alpha)
        elif kind == 1:
            x = _dil_mixer(x, b_w_qkv, b_w_out, j, rel_bias, ln_g[i, 0], ln_b[i, 0], alpha)
        else:
            x = _rwkv_mixer(x, j, c_mu[j], c_w_rkv, c_w0[j], c_w1, c_w2, c_a0[j],
                            c_a1, c_a2, c_g1, c_g2, c_k_k[j], c_k_a[j],
                            c_r_k[j], c_lnx_g[j], c_lnx_b[j], c_w_out,
                            ln_g[i, 0], ln_b[i, 0], alpha)
        kv = _linear(mem2d, x_w_kv, i, MXU_DTYPE, 512, 1024).reshape(bsz, mlen, 2 * d)
        x = _cross_attention(x, kv, x_w_q, x_w_out, i, ln_g[i, 1], ln_b[i, 1], alpha)
        x = _mlp(x.reshape(bsz * s, d), m_w1, m_w2, i, ln_g[i, 2], ln_b[i, 2],
                 alpha).reshape(bsz, s, d)
    return x
```

```python
import functools
import math

import jax
import jax.numpy as jnp
from jax import lax
from jax.experimental import pallas as pl
from jax.experimental.pallas import tpu as pltpu

F32 = jnp.float32
BF16 = jnp.bfloat16
MXU_DTYPE = jnp.bfloat16

CONV_WIDTH = 31
DIL_GROUPS = ((128, 1), (512, 4), (2048, 16))
DIL_HEADS = 16
DIL_HEAD_DIM = 64
REL_BUCKETS = 32
REL_MAX_DIST = 2048
RWKV_HEAD = 64
RWKV_GN_EPS = 64e-5
XATTN_HEADS = 4
LN_EPS = 1e-5
NEG_BIG = -1e30

LANES = 128
SUBLANES = 8
VMEM_LIMIT_BYTES = 56 * 1024 * 1024

RWKV_CHUNK = 64
TRI_BASE = 8
SCAN_PAIRS = 8


def _cparams(*sem):
    return pltpu.CompilerParams(dimension_semantics=sem, vmem_limit_bytes=VMEM_LIMIT_BYTES)


def _mm(a, b):
    return jnp.dot(a.astype(MXU_DTYPE), b.astype(MXU_DTYPE), preferred_element_type=F32)


def _mm_nt(a, b):
    return lax.dot_general(a.astype(MXU_DTYPE), b.astype(MXU_DTYPE),
                           (((1,), (1,)), ((), ())), preferred_element_type=F32)


def _split3(x):
    hi = x.astype(BF16)
    r1 = x - hi.astype(F32)
    mid = r1.astype(BF16)
    lo = (r1 - mid.astype(F32)).astype(BF16)
    return hi, mid, lo


def _mm_exact_rhs(x, e, passes=3):
    pieces = _split3(x)[:passes]
    return sum(jnp.dot(p, e, preferred_element_type=F32) for p in pieces)


def _mm_exact_lhs(e, x):
    hi, mid, lo = _split3(x)
    dot = functools.partial(jnp.dot, preferred_element_type=F32)
    return dot(e, hi) + dot(e, mid) + dot(e, lo)


def _sigmoid(z):
    return 1.0 / (1.0 + jnp.exp(-z))


def _softplus(z):
    return jnp.maximum(z, 0.0) + jnp.log(1.0 + jnp.exp(-jnp.abs(z)))


def _layer_norm(z, g, b, eps=LN_EPS):
    mu = jnp.mean(z, axis=-1, keepdims=True)
    zc = z - mu
    var = jnp.mean(zc * zc, axis=-1, keepdims=True)
    return zc * lax.rsqrt(var + eps) * g + b


def _postnorm(alpha, x, h, g, b):
    return _layer_norm(alpha * x + h, g, b)


def _full(shape):
    nd = len(shape)
    return pl.BlockSpec(shape, lambda *_: (0,) * nd)


def _resident(shape):
    nd = len(shape)
    return pl.BlockSpec(shape, lambda *_: (0,) * nd, pipeline_mode=pl.Buffered(1))


def _layer(arr, layer):
    nd = arr.ndim - 1
    return pl.BlockSpec((None,) + arr.shape[1:], lambda *_: (layer,) + (0,) * nd,
                        pipeline_mode=pl.Buffered(1))


def _row(v):
    return v.reshape(1, -1)


def _linear_kernel(x_ref, w_ref, o_ref):
    o_ref[...] = _mm(x_ref[...], w_ref[...]).astype(o_ref.dtype)


def _linear(x2d, w, layer, out_dtype, tm, tn):
    m, k = x2d.shape
    n = w.shape[2]
    tm, tn = min(tm, m), min(tn, n)
    return pl.pallas_call(
        _linear_kernel,
        grid=(m // tm, n // tn),
        in_specs=[pl.BlockSpec((tm, k), lambda i, j: (i, 0)),
                  pl.BlockSpec((None, k, tn), lambda i, j: (layer, 0, j))],
        out_specs=pl.BlockSpec((tm, tn), lambda i, j: (i, j)),
        out_shape=jax.ShapeDtypeStruct((m, n), out_dtype),
        compiler_params=_cparams("parallel", "parallel"),
        name="linear",
    )(x2d, w)


def _xattn_kernel(x_ref, kv_ref, wq_ref, wo_ref, g_ref, b_ref, o_ref, oh_ref, *, heads, alpha):
    xb = x_ref[0]
    d = xb.shape[-1]
    e = d // heads
    q = (_mm(xb, wq_ref[...]) * (e ** -0.5)).astype(MXU_DTYPE)
    for h in range(heads):
        kh = kv_ref[0, :, h * e:(h + 1) * e]
        vh = kv_ref[0, :, d + h * e:d + (h + 1) * e]
        s = _mm_nt(q[:, h * e:(h + 1) * e], kh)
        m = jnp.max(s, axis=-1, keepdims=True)
        p = jnp.exp(s - m)
        den = jnp.sum(p, axis=-1, keepdims=True)
        oh_ref[:, h * e:(h + 1) * e] = _mm(p, vh) / den
    hout = _mm(oh_ref[...], wo_ref[...])
    o_ref[0] = _postnorm(alpha, xb, hout, g_ref[...], b_ref[...])


def _cross_attention(x, kv, wq, wo, layer, g, b, alpha, tm=1024):
    bsz, s, d = x.shape
    mlen = kv.shape[1]
    tm = min(tm, s)
    return pl.pallas_call(
        functools.partial(_xattn_kernel, heads=XATTN_HEADS, alpha=alpha),
        grid=(bsz, s // tm),
        in_specs=[pl.BlockSpec((1, tm, d), lambda bi, i: (bi, i, 0)),
                  pl.BlockSpec((1, mlen, 2 * d), lambda bi, i: (bi, 0, 0)),
                  _layer(wq, layer), _layer(wo, layer), _full((1, d)), _full((1, d))],
        out_specs=pl.BlockSpec((1, tm, d), lambda bi, i: (bi, i, 0)),
        out_shape=jax.ShapeDtypeStruct((bsz, s, d), F32),
        scratch_shapes=[pltpu.VMEM((tm, d), F32)],
        compiler_params=_cparams("parallel", "parallel"),
        name="cross_attention",
    )(x, kv, wq, wo, _row(g), _row(b))


def _mlp_kernel(x_ref, w1_ref, w2_ref, g_ref, b_ref, o_ref, *, fchunk, alpha):
    xb = x_ref[...]
    xm = xb.astype(MXU_DTYPE)
    ff = w1_ref.shape[1]
    acc = jnp.zeros(xb.shape, F32)
    for c in range(ff // fchunk):
        hid = _mm(xm, w1_ref[:, c * fchunk:(c + 1) * fchunk])
        hid = jnp.maximum(hid, 0.0)
        acc = acc + _mm(hid * hid, w2_ref[c * fchunk:(c + 1) * fchunk, :])
    o_ref[...] = _postnorm(alpha, xb, acc, g_ref[...], b_ref[...])


def _mlp(x2d, w1, w2, layer, g, b, alpha, tm=512, fchunk=1024):
    n, d = x2d.shape
    ff = w1.shape[2]
    tm = min(tm, n)
    return pl.pallas_call(
        functools.partial(_mlp_kernel, fchunk=min(fchunk, ff), alpha=alpha),
        grid=(n // tm,),
        in_specs=[pl.BlockSpec((tm, d), lambda i: (i, 0)),
                  _layer(w1, layer), _layer(w2, layer), _full((1, d)), _full((1, d))],
        out_specs=pl.BlockSpec((tm, d), lambda i: (i, 0)),
        out_shape=jax.ShapeDtypeStruct((n, d), F32),
        compiler_params=_cparams("parallel"),
        name="sq_relu_mlp",
    )(x2d, w1, w2, _row(g), _row(b))


CONV_HALO = 32
CONV_PARTS = 2


def _conv_kernel(x_ref, win_ref, bin_ref, dw_ref, dwb_ref, lng_ref, lnb_ref, wo_ref, bo_ref,
                 g_ref, b_ref, o_ref, cat_ref, sh_ref, *, alpha):
    i = pl.program_id(1)
    tm, d = x_ref.shape[1], x_ref.shape[2]

    @pl.when(i == 0)
    def _():
        cat_ref[0:CONV_HALO, :] = jnp.zeros((CONV_HALO, d), F32)

    @pl.when(i > 0)
    def _():
        cat_ref[0:CONV_HALO, :] = cat_ref[tm:tm + CONV_HALO, :]

    xb = x_ref[0]
    parts = sh_ref.shape[0]
    pm = tm // parts
    for part in range(parts):
        r0 = part * pm
        h = _mm(xb[r0:r0 + pm], win_ref[...]) + bin_ref[...]
        cat_ref[CONV_HALO + r0:CONV_HALO + r0 + pm, :] = h[:, :d] * _sigmoid(h[:, d:])
    off = CONV_HALO - (CONV_WIDTH - 1)
    for part in range(parts):
        r0 = part * pm
        acc = jnp.broadcast_to(dwb_ref[...], (pm, d))
        for phase in range(SUBLANES):
            taps = [j for j in range(CONV_WIDTH) if (off + j) % SUBLANES == phase]
            span = max(off + j for j in taps) - phase + pm
            if phase:
                sh_ref[part, 0:span, :] = cat_ref[r0 + phase:r0 + phase + span, :]
            for j in taps:
                a = off + j - phase
                rows = sh_ref[part, a:a + pm, :] if phase else cat_ref[r0 + a:r0 + a + pm, :]
                acc = acc + dw_ref[j:j + 1, :] * rows
        z = _layer_norm(acc, lng_ref[...], lnb_ref[...])
        z = z * _sigmoid(z)
        hout = _mm(z, wo_ref[...]) + bo_ref[...]
        o_ref[0, r0:r0 + pm, :] = _postnorm(alpha, xb[r0:r0 + pm], hout, g_ref[...], b_ref[...])


def _conv_mixer(x, w_in, layer, b_in, dw, dw_b, ln_g, ln_b, w_out, b_out, g, b, alpha, tm=512):
    bsz, s, d = x.shape
    tm = min(tm, s)
    pm = tm // CONV_PARTS
    return pl.pallas_call(
        functools.partial(_conv_kernel, alpha=alpha),
        grid=(bsz, s // tm),
        in_specs=[pl.BlockSpec((1, tm, d), lambda bi, i: (bi, i, 0)),
                  _layer(w_in, layer), _full((1, 2 * d)),
                  _full((CONV_WIDTH, d)), _full((1, d)), _full((1, d)), _full((1, d)),
                  _layer(w_out, layer), _full((1, d)), _full((1, d)), _full((1, d))],
        out_specs=pl.BlockSpec((1, tm, d), lambda bi, i: (bi, i, 0)),
        out_shape=jax.ShapeDtypeStruct((bsz, s, d), F32),
        scratch_shapes=[pltpu.VMEM((CONV_HALO + tm, d), F32),
                        pltpu.VMEM((CONV_PARTS, CONV_HALO + pm, d), F32)],
        compiler_params=_cparams("arbitrary", "arbitrary"),
        name="conv_module",
    )(x, w_in, _row(b_in), dw, _row(dw_b), _row(ln_g), _row(ln_b), w_out, _row(b_out),
      _row(g), _row(b))


def _t5_causal_bucket(dist):
    n = jnp.maximum(dist, 0)
    max_exact = REL_BUCKETS // 2
    nf = jnp.maximum(n, 1).astype(F32)
    large = max_exact + (jnp.log(nf / max_exact) / math.log(REL_MAX_DIST / max_exact)
                         * (REL_BUCKETS - max_exact)).astype(jnp.int32)
    large = jnp.minimum(large, REL_BUCKETS - 1)
    return jnp.where(n < max_exact, n, large)


def _qkv_kernel(x_ref, w_ref, o_ref, xs_ref, xp_ref, *, dil, qscale):
    tm = x_ref.shape[1]
    n = tm // dil
    if dil == 1:
        xp = x_ref[0]
    else:
        for j in range(xs_ref.shape[0]):
            cols = slice(j * LANES, (j + 1) * LANES)
            xs_ref[j] = x_ref[0, :, cols]
            for c in range(dil):
                xp_ref[c * n:(c + 1) * n, cols] = (
                    xs_ref[j, pl.ds(c, n, stride=dil), :].astype(xp_ref.dtype))
        xp = xp_ref[...]
    acc = _mm(xp, w_ref[...])
    nblk = o_ref.shape[2]
    for c in range(dil):
        for j in range(nblk):
            tile = acc[c * n:(c + 1) * n, j * LANES:(j + 1) * LANES]
            if 3 * j < nblk:
                tile = tile * qscale
            o_ref[0, c, j] = tile.astype(o_ref.dtype)


def _qkv_proj(x, w_qkv, layer, group, dil, tm=512):
    bsz, s, d = x.shape
    sub = s // dil
    ncol = w_qkv.shape[2] // len(DIL_GROUPS)
    nblk = ncol // LANES
    tm = min(tm, s)
    return pl.pallas_call(
        functools.partial(_qkv_kernel, dil=dil, qscale=DIL_HEAD_DIM ** -0.5),
        grid=(bsz, s // tm),
        in_specs=[pl.BlockSpec((1, tm, d), lambda bi, i: (bi, i, 0)),
                  pl.BlockSpec((None, d, ncol), lambda bi, i: (layer, 0, group),
                               pipeline_mode=pl.Buffered(1))],
        out_specs=pl.BlockSpec((1, dil, nblk, tm // dil, LANES), lambda bi, i: (bi, 0, 0, i, 0)),
        out_shape=jax.ShapeDtypeStruct((bsz, dil, nblk, sub, LANES), MXU_DTYPE),
        scratch_shapes=[pltpu.VMEM((d // LANES, tm, LANES), F32), pltpu.VMEM((tm, d), MXU_DTYPE)],
        compiler_params=_cparams("parallel", "parallel"),
        name="dil_qkv_proj",
    )(x, w_qkv)


def _dil_bias_kernel(bucket_ref, relb_ref, bias_ref, *, heads, blk):
    bkt = bucket_ref[...]
    kcol = lax.broadcasted_iota(jnp.int32, bkt.shape, 1)
    for h in range(heads):
        acc = jnp.full(bkt.shape, NEG_BIG, F32)
        for n in range(REL_BUCKETS):
            acc = jnp.where(bkt == n, relb_ref[n, h], acc)
        bias_ref[h] = acc
        bias_ref[heads + h] = jnp.where(kcol < blk, NEG_BIG, acc)


def _dil_attn_kernel(bias_ref, q_ref, k_ref, v_ref, kp_ref, vp_ref, o_ref, l_ref,
                     kc_ref, vc_ref, *, heads, blk):
    i = pl.program_id(2)
    tq = q_ref.shape[3]
    kc_ref[:, 0:blk, :] = kp_ref[0, 0]
    kc_ref[:, blk:blk + tq, :] = k_ref[0, 0]
    vc_ref[:, 0:blk, :] = vp_ref[0, 0]
    vc_ref[:, blk:blk + tq, :] = v_ref[0, 0]
    lane = lax.broadcasted_iota(jnp.int32, (blk, LANES), 1)
    low = lane < DIL_HEAD_DIM

    for jb in range(tq // blk):
        rows = slice(jb * blk, (jb + 1) * blk)
        keys = slice(jb * blk, (jb + 2) * blk)
        table = jnp.where(i == 0, heads, 0) if jb == 0 else 0

        def pair(hp, lse_all):
            q2 = q_ref[0, 0, hp, rows, :]
            k2 = kc_ref[hp, keys, :]
            v2 = vc_ref[hp, keys, :]
            outs = []
            for hh in range(2):
                h = 2 * hp + hh
                keep = low if hh == 0 else jnp.logical_not(low)
                qm = jnp.where(keep, q2, jnp.zeros_like(q2))
                s = _mm_nt(qm, k2) + bias_ref[table + h]
                m = jnp.max(s, axis=-1, keepdims=True)
                p = jnp.exp(s - m)
                den = jnp.sum(p, axis=-1, keepdims=True)
                outs.append(_mm(p, v2) / den)
                lse_all = jnp.where(lane == h, m + jnp.log(den), lse_all)
            o_ref[0, 0, hp, rows, :] = jnp.where(low, outs[0], outs[1]).astype(o_ref.dtype)
            return lse_all

        lse_all = jnp.zeros((blk, LANES), F32)
        for hp in range(heads // 2):
            lse_all = pair(hp, lse_all)
        l_ref[0, 0, rows, :] = lse_all


def _dil_attention_group(qkv, rel_bias, window, dil, tq=512):
    bsz, _, nblk, sub, _ = qkv.shape
    hb = nblk // 3
    blk = window // dil
    tq = min(tq, sub)
    nsub = tq // blk
    qi = jnp.arange(blk)[:, None]
    kj = jnp.arange(2 * blk)[None, :]
    rel = qi + blk - kj
    bucket = jnp.where((rel >= 0) & (rel <= blk), _t5_causal_bucket(rel * dil), -1).astype(jnp.int32)

    bias = pl.pallas_call(
        functools.partial(_dil_bias_kernel, heads=DIL_HEADS, blk=blk),
        in_specs=[pl.BlockSpec(memory_space=pltpu.VMEM), pl.BlockSpec(memory_space=pltpu.SMEM)],
        out_specs=pl.BlockSpec(memory_space=pltpu.VMEM),
        out_shape=jax.ShapeDtypeStruct((2 * DIL_HEADS, blk, 2 * blk), F32),
        name="dil_bias_table",
    )(bucket, rel_bias)

    def cur(which):
        return pl.BlockSpec((1, 1, hb, tq, LANES), lambda bi, c, i: (bi, c, which, i, 0))

    def prev(which):
        return pl.BlockSpec((1, 1, hb, blk, LANES),
                            lambda bi, c, i: (bi, c, which, jnp.maximum(i * nsub - 1, 0), 0))

    return pl.pallas_call(
        functools.partial(_dil_attn_kernel, heads=DIL_HEADS, blk=blk),
        grid=(bsz, dil, sub // tq),
        in_specs=[_resident((2 * DIL_HEADS, blk, 2 * blk)), cur(0), cur(1), cur(2), prev(1), prev(2)],
        out_specs=[pl.BlockSpec((1, 1, hb, tq, LANES), lambda bi, c, i: (bi, c, 0, i, 0)),
                   pl.BlockSpec((1, 1, tq, LANES), lambda bi, c, i: (bi, c, i, 0))],
        out_shape=[jax.ShapeDtypeStruct((bsz, dil, hb, sub, LANES), MXU_DTYPE),
                   jax.ShapeDtypeStruct((bsz, dil, sub, LANES), F32)],
        scratch_shapes=[pltpu.VMEM((hb, blk + tq, LANES), MXU_DTYPE),
                        pltpu.VMEM((hb, blk + tq, LANES), MXU_DTYPE)],
        compiler_params=_cparams("parallel", "parallel", "parallel"),
        name="dil_attention",
    )(bias, qkv, qkv, qkv, qkv, qkv)


def _dil_out_kernel(o1_ref, o2_ref, o3_ref, l1_ref, l2_ref, l3_ref, x_ref, e_ref, wo_ref, g_ref,
                    b_ref, out_ref, os_ref, ls_ref, *, alpha, dils):
    tm = x_ref.shape[1]
    o_refs, l_refs = (o1_ref, o2_ref, o3_ref), (l1_ref, l2_ref, l3_ref)

    def natural(dst_ref, src, dil):
        if dil == 1:
            return src(0).astype(F32)
        n = tm // dil
        for c in range(dil):
            dst_ref[pl.ds(c, n, stride=dil), :] = src(c).astype(F32)
        return dst_ref[...]

    l1, l2, l3 = [natural(ls_ref.at[g], lambda c, r=l_refs[g]: r[0, c], dils[g]) for g in range(3)]
    m = jnp.maximum(jnp.maximum(l1, l2), l3)
    es = [jnp.exp(l1 - m), jnp.exp(l2 - m), jnp.exp(l3 - m)]
    inv = 1.0 / (es[0] + es[1] + es[2])
    ws = [_mm_exact_rhs(e * inv, e_ref[...], passes=2) for e in es[:2]]
    cols = []
    for j in range(o1_ref.shape[2]):
        blk = slice(j * LANES, (j + 1) * LANES)
        o1, o2, o3 = [natural(os_ref.at[g, j], lambda c, r=o_refs[g], j=j: r[0, c, j], dils[g])
                      for g in range(3)]
        cols.append(o3 + ws[0][:, blk] * (o1 - o3) + ws[1][:, blk] * (o2 - o3))
    hout = _mm(jnp.concatenate(cols, axis=1), wo_ref[...])
    out_ref[0] = _postnorm(alpha, x_ref[0], hout, g_ref[...], b_ref[...])


def _dil_mixer(x, w_qkv, w_out, layer, rel_bias, g, b, alpha, tm=256):
    bsz, s, d = x.shape
    he = DIL_HEADS * DIL_HEAD_DIM
    hb = he // LANES
    dils = tuple(dil for _, dil in DIL_GROUPS)
    outs, lses = [], []
    for gi, (window, dil) in enumerate(DIL_GROUPS):
        qkv = _qkv_proj(x, w_qkv, layer, gi, dil)
        o, lse = _dil_attention_group(qkv, rel_bias, window, dil)
        outs.append(o)
        lses.append(lse)
    tm = min(tm, s)
    o_specs = [pl.BlockSpec((1, dil, hb, tm // dil, LANES), lambda bi, i: (bi, 0, 0, i, 0))
               for dil in dils]
    l_specs = [pl.BlockSpec((1, dil, tm // dil, LANES), lambda bi, i: (bi, 0, i, 0)) for dil in dils]
    tile = pl.BlockSpec((1, tm, d), lambda bi, i: (bi, i, 0))
    expand = (jnp.arange(LANES)[:, None] == jnp.arange(he)[None, :] // DIL_HEAD_DIM).astype(BF16)
    return pl.pallas_call(
        functools.partial(_dil_out_kernel, alpha=alpha, dils=dils),
        grid=(bsz, s // tm),
        in_specs=o_specs + l_specs + [tile, _full((LANES, he)), _layer(w_out, layer),
                                      _full((1, d)), _full((1, d))],
        out_specs=tile,
        out_shape=jax.ShapeDtypeStruct((bsz, s, d), F32),
        scratch_shapes=[pltpu.VMEM((len(dils), hb, tm, LANES), F32),
                        pltpu.VMEM((len(dils), tm, LANES), F32)],
        compiler_params=_cparams("parallel", "parallel"),
        name="dil_combine_out",
    )(*outs, *lses, x, expand, w_out, _row(g), _row(b))


def _rwkv_prep_kernel(x_ref, xh_ref, mu_ref, wrkv_ref, w0_ref, w1_ref, w2_ref, a0_ref, a1_ref,
                      a2_ref, g1_ref, g2_ref, kk_ref, ka_ref,
                      r_o, k_o, v_o, lw_o, kk_o, b_o, g_o):
    i = pl.program_id(1)
    x = x_ref[0]
    tm = x.shape[0]
    prev_row = jnp.where(i == 0, 0.0, xh_ref[0, SUBLANES - 1:SUBLANES, :])
    rows = lax.broadcasted_iota(jnp.int32, (tm, 1), 0)
    xprev = jnp.where(rows == 0, prev_row, pltpu.roll(x, 1, 0))
    xx = xprev - x

    def mix(j):
        return x + xx * mu_ref[j:j + 1, :]

    r = _mm(mix(0), wrkv_ref[0])
    k = _mm(mix(1), wrkv_ref[1])
    v = _mm(mix(2), wrkv_ref[2])
    wl = w0_ref[...] + _mm(jnp.tanh(_mm(mix(3), w1_ref[...])), w2_ref[...])
    w_log = -_softplus(-wl) - 0.5
    a = _sigmoid(a0_ref[...] + _mm(_mm(mix(4), a1_ref[...]), a2_ref[...]))
    g = _mm(_sigmoid(_mm(mix(5), g1_ref[...])), g2_ref[...])
    kk = k * kk_ref[...]
    r_o[0] = r.astype(r_o.dtype)
    k_o[0] = (k * (1.0 + (a - 1.0) * ka_ref[...])).astype(k_o.dtype)
    v_o[0] = v.astype(v_o.dtype)
    lw_o[0] = -jnp.exp(w_log)
    kk_o[0] = kk.astype(kk_o.dtype)
    b_o[0] = (kk * a).astype(b_o.dtype)
    g_o[0] = g.astype(g_o.dtype)


def _tri_inv(mats, blk):
    n = mats[0].shape[0]
    r = lax.broadcasted_iota(jnp.int32, (n, n), 0)
    c = lax.broadcasted_iota(jnp.int32, (n, n), 1)

    def same(bs):
        sh = int(math.log2(bs))
        return lax.shift_right_logical(r, sh) == lax.shift_right_logical(c, sh)

    base = same(TRI_BASE)
    eye = jnp.where(r == c, 1.0, 0.0)
    ps = [jnp.where(base, a, 0.0) for a in mats]
    ts = [eye + p for p in ps]
    ps = [_mm(p, p) for p in ps]
    m = 2
    while 2 * m < TRI_BASE:
        both = [_mm(p, jnp.concatenate([p, t], axis=1)) for p, t in zip(ps, ts)]
        ts = [t + x[:, n:] for t, x in zip(ts, both)]
        ps = [x[:, :n] for x in both]
        m *= 2
    ts = [t + _mm(p, t) for p, t in zip(ps, ts)]
    bs = TRI_BASE
    while bs < blk:
        sel = same(2 * bs) & jnp.logical_not(same(bs))
        nblk = n // bs
        rows = lambda z, i: z[i * bs:(i + 1) * bs]
        t_odd = [jnp.concatenate([rows(t, i) for i in range(1, nblk, 2)], axis=0) for t in ts]
        lo = [_mm(to, jnp.where(sel, a, 0.0)) for to, a in zip(t_odd, mats)]
        upd = [_mm(l, t) for l, t in zip(lo, ts)]
        ts = [jnp.concatenate([rows(t, i) + rows(u, i // 2) if i % 2 else rows(t, i)
                               for i in range(nblk)], axis=0)
              for t, u in zip(ts, upd)]
        bs *= 2
    return ts


def _rwkv_scan_kernel(r_ref, k_ref, v_ref, lw_ref, kk_ref, b_ref, lg_ref, lb_ref, rk_ref,
                      y_ref, st_ref, *, chunk):
    @pl.when(pl.program_id(2) == 0)
    def _():
        st_ref[...] = jnp.zeros_like(st_ref)

    L = chunk
    n2 = 2 * L
    tb, width = r_ref.shape[1], r_ref.shape[2]
    chunks = range(tb // L)
    pairs = range(width // LANES)
    probs = [(p, ch) for ch in chunks for p in pairs]
    sh = int(math.log2(L))
    lane = lax.broadcasted_iota(jnp.int32, (L, LANES), 1)
    head0 = lane < RWKV_HEAD
    rr = lax.broadcasted_iota(jnp.int32, (n2, n2), 0)
    cc = lax.broadcasted_iota(jnp.int32, (n2, n2), 1)
    rt, ct = rr & (L - 1), cc & (L - 1)
    strict, incl, eye = rt > ct, rt >= ct, rr == cc
    tr = lax.broadcasted_iota(jnp.int32, (tb, tb), 0)
    tc = lax.broadcasted_iota(jnp.int32, (tb, tb), 1)
    same_chunk = lax.shift_right_logical(tr, sh) == lax.shift_right_logical(tc, sh)
    head0_tb = lax.broadcasted_iota(jnp.int32, (tb, LANES), 1) < RWKV_HEAD

    def head_sum(z):
        out = []
        for p in pairs:
            zp = z[:, p * LANES:(p + 1) * LANES]
            s0 = jnp.sum(jnp.where(head0_tb, zp, 0.0), axis=-1, keepdims=True)
            s1 = jnp.sum(jnp.where(head0_tb, 0.0, zp), axis=-1, keepdims=True)
            out.append(jnp.where(head0_tb, s0, s1))
        return jnp.concatenate(out, axis=1)

    r, k, v = r_ref[0].astype(F32), k_ref[0].astype(F32), v_ref[0].astype(F32)
    lw, kk = lw_ref[0], kk_ref[0].astype(F32)
    inv_norm = 1.0 / jnp.maximum(jnp.sqrt(head_sum(kk * kk)), 1e-12)
    kn = kk * inv_norm
    b = b_ref[0].astype(F32) * inv_norm
    tri = jnp.where(same_chunk & (tr >= tc), 1.0, 0.0).astype(BF16)
    c = _mm_exact_lhs(tri, lw)
    c_end = jnp.concatenate(
        [jnp.broadcast_to(c[(ch + 1) * L - 1:(ch + 1) * L, :], (L, width)) for ch in chunks], axis=0)
    e_inv = jnp.exp(-c)
    e_end = jnp.exp(c_end - c)
    p_end = jnp.exp(c_end)
    a_t = -kn * jnp.exp(c - lw)
    r_t = r * jnp.exp(c)
    bh, kh = b * e_inv, k * e_inv
    bb, kb = b * e_end, k * e_end

    def stack(z, prob):
        p, ch = prob
        z = z[ch * L:(ch + 1) * L, p * LANES:(p + 1) * LANES]
        return jnp.concatenate([jnp.where(head0, z, 0.0), jnp.where(head0, 0.0, z)], axis=0)

    a_s = [stack(a_t, q) for q in probs]
    r_s = [stack(r_t, q) for q in probs]
    v_s = [stack(v, q) for q in probs]
    ms = [_mm_nt(jnp.concatenate([a, rs], axis=0),
                 jnp.concatenate([stack(bh, q), stack(kh, q)], axis=0))
          for a, rs, q in zip(a_s, r_s, probs)]
    ts = _tri_inv([jnp.where(strict, m[:n2, :n2], 0.0) for m in ms], L)
    av = [_mm(jnp.where(strict, m[:n2, n2:], 0.0), vs) for m, vs in zip(ms, v_s)]
    tu = [_mm(t, jnp.concatenate([a, x], axis=1)) for t, a, x in zip(ts, a_s, av)]
    zeros = jnp.zeros((n2, LANES), F32)
    incl2 = jnp.concatenate([incl, incl], axis=1)
    big = [_mm(jnp.concatenate([jnp.where(incl2, m[n2:], 0.0),
                                jnp.concatenate([stack(bb, q), stack(kb, q)], axis=0).T], axis=0),
               jnp.concatenate([x, jnp.concatenate([zeros, vs], axis=1)], axis=0))
           for m, q, x, vs in zip(ms, probs, tu, v_s)]
    rg_lhs, y0, h0 = {}, {}, {}
    for i, (p, ch) in enumerate(probs):
        p_last = p_end[ch * L:ch * L + 1, p * LANES:(p + 1) * LANES]
        rg_lhs[p, ch] = big[i][:, :LANES] + jnp.concatenate(
            [r_s[i], jnp.where(eye, p_last, 0.0)], axis=0)
        y0[p, ch] = big[i][:n2, LANES:]
        h0[p, ch] = big[i][n2:, LANES:]

    st = [st_ref[p] for p in pairs]
    ys = [[] for _ in pairs]
    for ch in chunks:
        for p in pairs:
            rg = _mm(rg_lhs[p, ch], st[p])
            y2 = rg[:n2] + y0[p, ch]
            st[p] = rg[n2:] + h0[p, ch]
            ys[p].append(y2[:L] + y2[L:])
    for p in pairs:
        st_ref[p] = st[p]

    y = jnp.concatenate([jnp.concatenate(yp, axis=0) for yp in ys], axis=1)
    inv_n = 1.0 / RWKV_HEAD
    yc = y - head_sum(y) * inv_n
    var = head_sum(yc * yc) * inv_n
    yn = yc * lax.rsqrt(var + RWKV_GN_EPS) * lg_ref[...] + lb_ref[...]
    y_ref[0] = yn + head_sum(r * k * rk_ref[...]) * v


def _rwkv_post_kernel(y_ref, gate_ref, x_ref, wo_ref, g_ref, b_ref, o_ref, *, alpha):
    hout = _mm(y_ref[...] * gate_ref[...], wo_ref[...])
    o_ref[...] = _postnorm(alpha, x_ref[...], hout, g_ref[...], b_ref[...])


def _rwkv_mixer(x, layer, mu, w_rkv, w0, w1, w2, a0, a1, a2, g1, g2, k_k, k_a, r_k, lnx_g, lnx_b,
                w_out, g, b, alpha, tm=512, tblk=128):
    bsz, s, d = x.shape
    tm = min(tm, s)
    tile = pl.BlockSpec((1, tm, d), lambda bi, i: (bi, i, 0))
    hb = tm // SUBLANES
    outs = pl.pallas_call(
        _rwkv_prep_kernel,
        grid=(bsz, s // tm),
        in_specs=[tile,
                  pl.BlockSpec((1, SUBLANES, d), lambda bi, i: (bi, jnp.maximum(i * hb - 1, 0), 0)),
                  _full((6, d)), _layer(w_rkv, layer), _full((1, d)),
                  _layer(w1, layer), _layer(w2, layer), _full((1, d)),
                  _layer(a1, layer), _layer(a2, layer),
                  _layer(g1, layer), _layer(g2, layer),
                  _full((1, d)), _full((1, d))],
        out_specs=[tile] * 7,
        out_shape=[jax.ShapeDtypeStruct((bsz, s, d), F32 if name == "lw" else MXU_DTYPE)
                   for name in ("r", "k", "v", "lw", "kk", "b", "gate")],
        compiler_params=_cparams("parallel", "parallel"),
        name="rwkv_prep",
    )(x, x, mu, w_rkv, _row(w0), w1, w2, _row(a0), a1, a2, g1, g2, _row(k_k), _row(k_a))
    r, k, v, lw, kk, bvec, gate = outs

    tblk = min(tblk, s)
    width = SCAN_PAIRS * LANES
    blk = pl.BlockSpec((1, tblk, width), lambda bi, hp, i: (bi, i, hp))
    chan = pl.BlockSpec((1, width), lambda bi, hp, i: (0, hp))
    y = pl.pallas_call(
        functools.partial(_rwkv_scan_kernel, chunk=RWKV_CHUNK),
        grid=(bsz, d // width, s // tblk),
        in_specs=[blk] * 6 + [chan] * 3,
        out_specs=blk,
        out_shape=jax.ShapeDtypeStruct((bsz, s, d), F32),
        scratch_shapes=[pltpu.VMEM((SCAN_PAIRS, LANES, LANES), F32)],
        compiler_params=_cparams("parallel", "parallel", "arbitrary"),
        name="rwkv_scan",
    )(r, k, v, lw, kk, bvec, _row(lnx_g), _row(lnx_b), _row(r_k))

    n = bsz * s
    tm2 = min(tm, n)
    row = pl.BlockSpec((tm2, d), lambda i: (i, 0))
    flat = lambda z: z.reshape(n, d)
    out = pl.pallas_call(
        functools.partial(_rwkv_post_kernel, alpha=alpha),
        grid=(n // tm2,),
        in_specs=[row] * 3 + [_layer(w_out, layer), _full((1, d)), _full((1, d))],
        out_specs=row,
        out_shape=jax.ShapeDtypeStruct((n, d), F32),
        compiler_params=_cparams("parallel"),
        name="rwkv_post",
    )(flat(y), flat(gate), flat(x), w_out, _row(g), _row(b))
    return out.reshape(bsz, s, d)


def kernel(x, mem, rel_bias, a_w_in, a_b_in, a_dw, a_dw_b, a_ln_g, a_ln_b, a_w_out, a_b_out, b_w_qkv, b_w_out, c_mu, c_w_rkv, c_w0, c_w1, c_w2, c_a0, c_a1, c_a2, c_g1, c_g2, c_k_k, c_k_a, c_r_k, c_lnx_g, c_lnx_b, c_w_out, x_w_q, x_w_kv, x_w_out, m_w1, m_w2, ln_g, ln_b):
    depth = ln_g.shape[0]
    alpha = (2 * depth) ** 0.25
    bsz, s, d = x.shape
    mlen = mem.shape[1]
    mem2d = mem.reshape(bsz * mlen, d)
    for i in range(depth):
        kind, j = i % 3, i // 3
        if kind == 0:
            x = _conv_mixer(x, a_w_in, j, a_b_in[j], a_dw[j], a_dw_b[j], a_ln_g[j], a_ln_b[j],
                            a_w_out, a_b_out[j], ln_g[i, 0], ln_b[i, 0], alpha)
        elif kind == 1:
            x = _dil_mixer(x, b_w_qkv, b_w_out, j, rel_bias, ln_g[i, 0], ln_b[i, 0], alpha)
        else:
            x = _rwkv_mixer(x, j, c_mu[j], c_w_rkv, c_w0[j], c_w1, c_w2, c_a0[j],
                            c_a1, c_a2, c_g1, c_g2, c_k_k[j], c_k_a[j],
                            c_r_k[j], c_lnx_g[j], c_lnx_b[j], c_w_out,
                            ln_g[i, 0], ln_b[i, 0], alpha)
        kv = _linear(mem2d, x_w_kv, i, MXU_DTYPE, 512, 1024).reshape(bsz, mlen, 2 * d)
        x = _cross_attention(x, kv, x_w_q, x_w_out, i, ln_g[i, 1], ln_b[i, 1], alpha)
        x = _mlp(x.reshape(bsz * s, d), m_w1, m_w2, i, ln_g[i, 2], ln_b[i, 2],
                 alpha).reshape(bsz, s, d)
    return x
```

```python
import functools
import math

import jax
import jax.numpy as jnp
from jax import lax
from jax.experimental import pallas as pl
from jax.experimental.pallas import tpu as pltpu

F32 = jnp.float32
BF16 = jnp.bfloat16
MXU_DTYPE = jnp.bfloat16

CONV_WIDTH = 31
DIL_GROUPS = ((128, 1), (512, 4), (2048, 16))
DIL_HEADS = 16
DIL_HEAD_DIM = 64
REL_BUCKETS = 32
REL_MAX_DIST = 2048
RWKV_HEAD = 64
RWKV_GN_EPS = 64e-5
XATTN_HEADS = 4
LN_EPS = 1e-5
NEG_BIG = -1e30

LANES = 128
SUBLANES = 8
VMEM_LIMIT_BYTES = 56 * 1024 * 1024

RWKV_CHUNK = 64
TRI_BASE = 8
SCAN_PAIRS = 8


def _cparams(*sem):
    return pltpu.CompilerParams(dimension_semantics=sem, vmem_limit_bytes=VMEM_LIMIT_BYTES)


def _mm(a, b):
    return jnp.dot(a.astype(MXU_DTYPE), b.astype(MXU_DTYPE), preferred_element_type=F32)


def _mm_nt(a, b):
    return lax.dot_general(a.astype(MXU_DTYPE), b.astype(MXU_DTYPE),
                           (((1,), (1,)), ((), ())), preferred_element_type=F32)


def _split3(x):
    hi = x.astype(BF16)
    r1 = x - hi.astype(F32)
    mid = r1.astype(BF16)
    lo = (r1 - mid.astype(F32)).astype(BF16)
    return hi, mid, lo


def _mm_exact_rhs(x, e, passes=3):
    pieces = _split3(x)[:passes]
    return sum(jnp.dot(p, e, preferred_element_type=F32) for p in pieces)


def _mm_exact_lhs(e, x):
    hi, mid, lo = _split3(x)
    dot = functools.partial(jnp.dot, preferred_element_type=F32)
    return dot(e, hi) + dot(e, mid) + dot(e, lo)


def _sigmoid(z):
    return 1.0 / (1.0 + jnp.exp(-z))


def _layer_norm(z, g, b, eps=LN_EPS):
    mu = jnp.mean(z, axis=-1, keepdims=True)
    zc = z - mu
    var = jnp.mean(zc * zc, axis=-1, keepdims=True)
    return zc * lax.rsqrt(var + eps) * g + b


def _postnorm(alpha, x, h, g, b):
    return _layer_norm(alpha * x + h, g, b)


def _full(shape):
    nd = len(shape)
    return pl.BlockSpec(shape, lambda *_: (0,) * nd)


def _resident(shape):
    nd = len(shape)
    return pl.BlockSpec(shape, lambda *_: (0,) * nd, pipeline_mode=pl.Buffered(1))


def _layer(arr, layer):
    nd = arr.ndim - 1
    return pl.BlockSpec((None,) + arr.shape[1:], lambda *_: (layer,) + (0,) * nd,
                        pipeline_mode=pl.Buffered(1))


def _row(v):
    return v.reshape(1, -1)


def _linear_kernel(x_ref, w_ref, o_ref):
    o_ref[...] = _mm(x_ref[...], w_ref[...]).astype(o_ref.dtype)


def _linear(x2d, w, layer, out_dtype, tm, tn):
    m, k = x2d.shape
    n = w.shape[2]
    tm, tn = min(tm, m), min(tn, n)
    return pl.pallas_call(
        _linear_kernel,
        grid=(m // tm, n // tn),
        in_specs=[pl.BlockSpec((tm, k), lambda i, j: (i, 0)),
                  pl.BlockSpec((None, k, tn), lambda i, j: (layer, 0, j))],
        out_specs=pl.BlockSpec((tm, tn), lambda i, j: (i, j)),
        out_shape=jax.ShapeDtypeStruct((m, n), out_dtype),
        compiler_params=_cparams("parallel", "parallel"),
        name="linear",
    )(x2d, w)


def _xattn_kernel(x_ref, kv_ref, wq_ref, wo_ref, g_ref, b_ref, o_ref, oh_ref, *, heads, alpha):
    xb = x_ref[0]
    d = xb.shape[-1]
    e = d // heads
    q = (_mm(xb, wq_ref[...]) * (e ** -0.5)).astype(MXU_DTYPE)
    for h in range(heads):
        kh = kv_ref[0, :, h * e:(h + 1) * e]
        vh = kv_ref[0, :, d + h * e:d + (h + 1) * e]
        s = _mm_nt(q[:, h * e:(h + 1) * e], kh)
        m = jnp.max(s, axis=-1, keepdims=True)
        p = jnp.exp(s - m)
        den = jnp.sum(p, axis=-1, keepdims=True)
        oh_ref[:, h * e:(h + 1) * e] = _mm(p, vh) / den
    hout = _mm(oh_ref[...], wo_ref[...])
    o_ref[0] = _postnorm(alpha, xb, hout, g_ref[...], b_ref[...])


def _cross_attention(x, kv, wq, wo, layer, g, b, alpha, tm=1024):
    bsz, s, d = x.shape
    mlen = kv.shape[1]
    tm = min(tm, s)
    return pl.pallas_call(
        functools.partial(_xattn_kernel, heads=XATTN_HEADS, alpha=alpha),
        grid=(bsz, s // tm),
        in_specs=[pl.BlockSpec((1, tm, d), lambda bi, i: (bi, i, 0)),
                  pl.BlockSpec((1, mlen, 2 * d), lambda bi, i: (bi, 0, 0)),
                  _layer(wq, layer), _layer(wo, layer), _full((1, d)), _full((1, d))],
        out_specs=pl.BlockSpec((1, tm, d), lambda bi, i: (bi, i, 0)),
        out_shape=jax.ShapeDtypeStruct((bsz, s, d), F32),
        scratch_shapes=[pltpu.VMEM((tm, d), F32)],
        compiler_params=_cparams("parallel", "parallel"),
        name="cross_attention",
    )(x, kv, wq, wo, _row(g), _row(b))


def _mlp_kernel(x_ref, w1_ref, w2_ref, g_ref, b_ref, o_ref, *, fchunk, alpha):
    xb = x_ref[...]
    xm = xb.astype(MXU_DTYPE)
    ff = w1_ref.shape[1]
    acc = jnp.zeros(xb.shape, F32)
    for c in range(ff // fchunk):
        hid = _mm(xm, w1_ref[:, c * fchunk:(c + 1) * fchunk])
        hid = jnp.maximum(hid, 0.0)
        acc = acc + _mm(hid * hid, w2_ref[c * fchunk:(c + 1) * fchunk, :])
    o_ref[...] = _postnorm(alpha, xb, acc, g_ref[...], b_ref[...])


def _mlp(x2d, w1, w2, layer, g, b, alpha, tm=512, fchunk=1024):
    n, d = x2d.shape
    ff = w1.shape[2]
    tm = min(tm, n)
    return pl.pallas_call(
        functools.partial(_mlp_kernel, fchunk=min(fchunk, ff), alpha=alpha),
        grid=(n // tm,),
        in_specs=[pl.BlockSpec((tm, d), lambda i: (i, 0)),
                  _layer(w1, layer), _layer(w2, layer), _full((1, d)), _full((1, d))],
        out_specs=pl.BlockSpec((tm, d), lambda i: (i, 0)),
        out_shape=jax.ShapeDtypeStruct((n, d), F32),
        compiler_params=_cparams("parallel"),
        name="sq_relu_mlp",
    )(x2d, w1, w2, _row(g), _row(b))


CONV_HALO = 32
CONV_PARTS = 2


def _conv_kernel(x_ref, win_ref, bin_ref, dw_ref, dwb_ref, lng_ref, lnb_ref, wo_ref, bo_ref,
                 g_ref, b_ref, o_ref, cat_ref, sh_ref, *, alpha):
    i = pl.program_id(1)
    tm, d = x_ref.shape[1], x_ref.shape[2]

    @pl.when(i == 0)
    def _():
        cat_ref[0:CONV_HALO, :] = jnp.zeros((CONV_HALO, d), F32)

    @pl.when(i > 0)
    def _():
        cat_ref[0:CONV_HALO, :] = cat_ref[tm:tm + CONV_HALO, :]

    xb = x_ref[0]
    parts = sh_ref.shape[0]
    pm = tm // parts
    for part in range(parts):
        r0 = part * pm
        h = _mm(xb[r0:r0 + pm], win_ref[...]) + bin_ref[...]
        cat_ref[CONV_HALO + r0:CONV_HALO + r0 + pm, :] = h[:, :d] * _sigmoid(h[:, d:])
    off = CONV_HALO - (CONV_WIDTH - 1)
    for part in range(parts):
        r0 = part * pm
        acc = jnp.broadcast_to(dwb_ref[...], (pm, d))
        for phase in range(SUBLANES):
            taps = [j for j in range(CONV_WIDTH) if (off + j) % SUBLANES == phase]
            span = max(off + j for j in taps) - phase + pm
            if phase:
                sh_ref[part, 0:span, :] = cat_ref[r0 + phase:r0 + phase + span, :]
            for j in taps:
                a = off + j - phase
                rows = sh_ref[part, a:a + pm, :] if phase else cat_ref[r0 + a:r0 + a + pm, :]
                acc = acc + dw_ref[j:j + 1, :] * rows
        z = _layer_norm(acc, lng_ref[...], lnb_ref[...])
        z = z * _sigmoid(z)
        hout = _mm(z, wo_ref[...]) + bo_ref[...]
        o_ref[0, r0:r0 + pm, :] = _postnorm(alpha, xb[r0:r0 + pm], hout, g_ref[...], b_ref[...])


def _conv_mixer(x, w_in, layer, b_in, dw, dw_b, ln_g, ln_b, w_out, b_out, g, b, alpha, tm=512):
    bsz, s, d = x.shape
    tm = min(tm, s)
    pm = tm // CONV_PARTS
    return pl.pallas_call(
        functools.partial(_conv_kernel, alpha=alpha),
        grid=(bsz, s // tm),
        in_specs=[pl.BlockSpec((1, tm, d), lambda bi, i: (bi, i, 0)),
                  _layer(w_in, layer), _full((1, 2 * d)),
                  _full((CONV_WIDTH, d)), _full((1, d)), _full((1, d)), _full((1, d)),
                  _layer(w_out, layer), _full((1, d)), _full((1, d)), _full((1, d))],
        out_specs=pl.BlockSpec((1, tm, d), lambda bi, i: (bi, i, 0)),
        out_shape=jax.ShapeDtypeStruct((bsz, s, d), F32),
        scratch_shapes=[pltpu.VMEM((CONV_HALO + tm, d), F32),
                        pltpu.VMEM((CONV_PARTS, CONV_HALO + pm, d), F32)],
        compiler_params=_cparams("arbitrary", "arbitrary"),
        name="conv_module",
    )(x, w_in, _row(b_in), dw, _row(dw_b), _row(ln_g), _row(ln_b), w_out, _row(b_out),
      _row(g), _row(b))


def _t5_causal_bucket(dist):
    n = jnp.maximum(dist, 0)
    max_exact = REL_BUCKETS // 2
    nf = jnp.maximum(n, 1).astype(F32)
    large = max_exact + (jnp.log(nf / max_exact) / math.log(REL_MAX_DIST / max_exact)
                         * (REL_BUCKETS - max_exact)).astype(jnp.int32)
    large = jnp.minimum(large, REL_BUCKETS - 1)
    return jnp.where(n < max_exact, n, large)


def _qkv_kernel(x_ref, w_ref, o_ref, xs_ref, xp_ref, *, dil, qscale):
    tm = x_ref.shape[1]
    n = tm // dil
    if dil == 1:
        xp = x_ref[0]
    else:
        for j in range(xs_ref.shape[0]):
            cols = slice(j * LANES, (j + 1) * LANES)
            xs_ref[j] = x_ref[0, :, cols]
            for c in range(dil):
                xp_ref[c * n:(c + 1) * n, cols] = (
                    xs_ref[j, pl.ds(c, n, stride=dil), :].astype(xp_ref.dtype))
        xp = xp_ref[...]
    acc = _mm(xp, w_ref[...])
    nblk = o_ref.shape[2]
    for c in range(dil):
        for j in range(nblk):
            tile = acc[c * n:(c + 1) * n, j * LANES:(j + 1) * LANES]
            if 3 * j < nblk:
                tile = tile * qscale
            o_ref[0, c, j] = tile.astype(o_ref.dtype)


def _qkv_proj(x, w_qkv, layer, group, dil, tm=512):
    bsz, s, d = x.shape
    sub = s // dil
    ncol = w_qkv.shape[2] // len(DIL_GROUPS)
    nblk = ncol // LANES
    tm = min(tm, s)
    return pl.pallas_call(
        functools.partial(_qkv_kernel, dil=dil, qscale=DIL_HEAD_DIM ** -0.5),
        grid=(bsz, s // tm),
        in_specs=[pl.BlockSpec((1, tm, d), lambda bi, i: (bi, i, 0)),
                  pl.BlockSpec((None, d, ncol), lambda bi, i: (layer, 0, group),
                               pipeline_mode=pl.Buffered(1))],
        out_specs=pl.BlockSpec((1, dil, nblk, tm // dil, LANES), lambda bi, i: (bi, 0, 0, i, 0)),
        out_shape=jax.ShapeDtypeStruct((bsz, dil, nblk, sub, LANES), MXU_DTYPE),
        scratch_shapes=[pltpu.VMEM((d // LANES, tm, LANES), F32), pltpu.VMEM((tm, d), MXU_DTYPE)],
        compiler_params=_cparams("parallel", "parallel"),
        name="dil_qkv_proj",
    )(x, w_qkv)


def _dil_bias_kernel(bucket_ref, relb_ref, bias_ref, *, heads, blk):
    bkt = bucket_ref[...]
    kcol = lax.broadcasted_iota(jnp.int32, bkt.shape, 1)
    for h in range(heads):
        acc = jnp.full(bkt.shape, NEG_BIG, F32)
        for n in range(REL_BUCKETS):
            acc = jnp.where(bkt == n, relb_ref[n, h], acc)
        bias_ref[h] = acc
        bias_ref[heads + h] = jnp.where(kcol < blk, NEG_BIG, acc)


def _dil_attn_kernel(bias_ref, q_ref, k_ref, v_ref, kp_ref, vp_ref, o_ref, l_ref,
                     kc_ref, vc_ref, *, heads, blk):
    i = pl.program_id(2)
    tq = q_ref.shape[3]
    kc_ref[:, 0:blk, :] = kp_ref[0, 0]
    kc_ref[:, blk:blk + tq, :] = k_ref[0, 0]
    vc_ref[:, 0:blk, :] = vp_ref[0, 0]
    vc_ref[:, blk:blk + tq, :] = v_ref[0, 0]
    lane = lax.broadcasted_iota(jnp.int32, (blk, LANES), 1)
    low = lane < DIL_HEAD_DIM

    for jb in range(tq // blk):
        rows = slice(jb * blk, (jb + 1) * blk)
        keys = slice(jb * blk, (jb + 2) * blk)
        table = jnp.where(i == 0, heads, 0) if jb == 0 else 0

        def pair(hp, lse_all):
            q2 = q_ref[0, 0, hp, rows, :]
            k2 = kc_ref[hp, keys, :]
            v2 = vc_ref[hp, keys, :]
            outs = []
            for hh in range(2):
                h = 2 * hp + hh
                keep = low if hh == 0 else jnp.logical_not(low)
                qm = jnp.where(keep, q2, jnp.zeros_like(q2))
                s = _mm_nt(qm, k2) + bias_ref[table + h]
                m = jnp.max(s, axis=-1, keepdims=True)
                p = jnp.exp(s - m)
                den = jnp.sum(p, axis=-1, keepdims=True)
                outs.append(_mm(p, v2) / den)
                lse_all = jnp.where(lane == h, m + jnp.log(den), lse_all)
            o_ref[0, 0, hp, rows, :] = jnp.where(low, outs[0], outs[1]).astype(o_ref.dtype)
            return lse_all

        lse_all = jnp.zeros((blk, LANES), F32)
        for hp in range(heads // 2):
            lse_all = pair(hp, lse_all)
        l_ref[0, 0, rows, :] = lse_all


def _dil_attention_group(qkv, rel_bias, window, dil, tq=1024):
    bsz, _, nblk, sub, _ = qkv.shape
    hb = nblk // 3
    blk = window // dil
    tq = min(tq, sub)
    nsub = tq // blk
    qi = jnp.arange(blk)[:, None]
    kj = jnp.arange(2 * blk)[None, :]
    rel = qi + blk - kj
    bucket = jnp.where((rel >= 0) & (rel <= blk), _t5_causal_bucket(rel * dil), -1).astype(jnp.int32)

    bias = pl.pallas_call(
        functools.partial(_dil_bias_kernel, heads=DIL_HEADS, blk=blk),
        in_specs=[pl.BlockSpec(memory_space=pltpu.VMEM), pl.BlockSpec(memory_space=pltpu.SMEM)],
        out_specs=pl.BlockSpec(memory_space=pltpu.VMEM),
        out_shape=jax.ShapeDtypeStruct((2 * DIL_HEADS, blk, 2 * blk), F32),
        name="dil_bias_table",
    )(bucket, rel_bias)

    def cur(which):
        return pl.BlockSpec((1, 1, hb, tq, LANES), lambda bi, c, i: (bi, c, which, i, 0))

    def prev(which):
        return pl.BlockSpec((1, 1, hb, blk, LANES),
                            lambda bi, c, i: (bi, c, which, jnp.maximum(i * nsub - 1, 0), 0))

    return pl.pallas_call(
        functools.partial(_dil_attn_kernel, heads=DIL_HEADS, blk=blk),
        grid=(bsz, dil, sub // tq),
        in_specs=[_resident((2 * DIL_HEADS, blk, 2 * blk)), cur(0), cur(1), cur(2), prev(1), prev(2)],
        out_specs=[pl.BlockSpec((1, 1, hb, tq, LANES), lambda bi, c, i: (bi, c, 0, i, 0)),
                   pl.BlockSpec((1, 1, tq, LANES), lambda bi, c, i: (bi, c, i, 0))],
        out_shape=[jax.ShapeDtypeStruct((bsz, dil, hb, sub, LANES), MXU_DTYPE),
                   jax.ShapeDtypeStruct((bsz, dil, sub, LANES), F32)],
        scratch_shapes=[pltpu.VMEM((hb, blk + tq, LANES), MXU_DTYPE),
                        pltpu.VMEM((hb, blk + tq, LANES), MXU_DTYPE)],
        compiler_params=_cparams("parallel", "parallel", "parallel"),
        name="dil_attention",
    )(bias, qkv, qkv, qkv, qkv, qkv)


def _dil_out_kernel(o1_ref, o2_ref, o3_ref, l1_ref, l2_ref, l3_ref, x_ref, e_ref, wo_ref, g_ref,
                    b_ref, out_ref, os_ref, ls_ref, *, alpha, dils):
    tm = x_ref.shape[1]
    o_refs, l_refs = (o1_ref, o2_ref, o3_ref), (l1_ref, l2_ref, l3_ref)

    def natural(dst_ref, src, dil):
        if dil == 1:
            return src(0).astype(F32)
        n = tm // dil
        for c in range(dil):
            dst_ref[pl.ds(c, n, stride=dil), :] = src(c).astype(F32)
        return dst_ref[...]

    l1, l2, l3 = [natural(ls_ref.at[g], lambda c, r=l_refs[g]: r[0, c], dils[g]) for g in range(3)]
    m = jnp.maximum(jnp.maximum(l1, l2), l3)
    es = [jnp.exp(l1 - m), jnp.exp(l2 - m), jnp.exp(l3 - m)]
    inv = 1.0 / (es[0] + es[1] + es[2])
    ws = [_mm_exact_rhs(e * inv, e_ref[...], passes=2) for e in es[:2]]
    cols = []
    for j in range(o1_ref.shape[2]):
        blk = slice(j * LANES, (j + 1) * LANES)
        o1, o2, o3 = [natural(os_ref.at[g, j], lambda c, r=o_refs[g], j=j: r[0, c, j], dils[g])
                      for g in range(3)]
        cols.append(o3 + ws[0][:, blk] * (o1 - o3) + ws[1][:, blk] * (o2 - o3))
    hout = _mm(jnp.concatenate(cols, axis=1), wo_ref[...])
    out_ref[0] = _postnorm(alpha, x_ref[0], hout, g_ref[...], b_ref[...])


def _dil_mixer(x, w_qkv, w_out, layer, rel_bias, g, b, alpha, tm=512):
    bsz, s, d = x.shape
    he = DIL_HEADS * DIL_HEAD_DIM
    hb = he // LANES
    dils = tuple(dil for _, dil in DIL_GROUPS)
    outs, lses = [], []
    for gi, (window, dil) in enumerate(DIL_GROUPS):
        qkv = _qkv_proj(x, w_qkv, layer, gi, dil)
        o, lse = _dil_attention_group(qkv, rel_bias, window, dil)
        outs.append(o)
        lses.append(lse)
    tm = min(tm, s)
    o_specs = [pl.BlockSpec((1, dil, hb, tm // dil, LANES), lambda bi, i: (bi, 0, 0, i, 0))
               for dil in dils]
    l_specs = [pl.BlockSpec((1, dil, tm // dil, LANES), lambda bi, i: (bi, 0, i, 0)) for dil in dils]
    tile = pl.BlockSpec((1, tm, d), lambda bi, i: (bi, i, 0))
    expand = (jnp.arange(LANES)[:, None] == jnp.arange(he)[None, :] // DIL_HEAD_DIM).astype(BF16)
    return pl.pallas_call(
        functools.partial(_dil_out_kernel, alpha=alpha, dils=dils),
        grid=(bsz, s // tm),
        in_specs=o_specs + l_specs + [tile, _full((LANES, he)), _layer(w_out, layer),
                                      _full((1, d)), _full((1, d))],
        out_specs=tile,
        out_shape=jax.ShapeDtypeStruct((bsz, s, d), F32),
        scratch_shapes=[pltpu.VMEM((len(dils), hb, tm, LANES), F32),
                        pltpu.VMEM((len(dils), tm, LANES), F32)],
        compiler_params=_cparams("parallel", "parallel"),
        name="dil_combine_out",
    )(*outs, *lses, x, expand, w_out, _row(g), _row(b))


def _rwkv_prep_kernel(x_ref, xh_ref, mu_ref, wrkv_ref, w0_ref, w1_ref, w2_ref, a0_ref, a1_ref,
                      a2_ref, g1_ref, g2_ref, kk_ref, ka_ref,
                      r_o, k_o, v_o, lw_o, kk_o, b_o, g_o):
    i = pl.program_id(1)
    x = x_ref[0]
    tm = x.shape[0]
    prev_row = jnp.where(i == 0, 0.0, xh_ref[0, SUBLANES - 1:SUBLANES, :])
    rows = lax.broadcasted_iota(jnp.int32, (tm, 1), 0)
    xprev = jnp.where(rows == 0, prev_row, pltpu.roll(x, 1, 0))
    xx = xprev - x

    def mix(j):
        return x + xx * mu_ref[j:j + 1, :]

    r = _mm(mix(0), wrkv_ref[0])
    k = _mm(mix(1), wrkv_ref[1])
    v = _mm(mix(2), wrkv_ref[2])
    wl = w0_ref[...] + _mm(jnp.tanh(_mm(mix(3), w1_ref[...])), w2_ref[...])
    a = _sigmoid(a0_ref[...] + _mm(_mm(mix(4), a1_ref[...]), a2_ref[...]))
    g = _mm(_sigmoid(_mm(mix(5), g1_ref[...])), g2_ref[...])
    kk = k * kk_ref[...]
    r_o[0] = r.astype(r_o.dtype)
    k_o[0] = (k * (1.0 + (a - 1.0) * ka_ref[...])).astype(k_o.dtype)
    v_o[0] = v.astype(v_o.dtype)
    lw_o[0] = -math.exp(-0.5) * _sigmoid(wl)
    kk_o[0] = kk.astype(kk_o.dtype)
    b_o[0] = (kk * a).astype(b_o.dtype)
    g_o[0] = g.astype(g_o.dtype)


def _tri_inv(mats, blk):
    n = mats[0].shape[0]
    r = lax.broadcasted_iota(jnp.int32, (n, n), 0)
    c = lax.broadcasted_iota(jnp.int32, (n, n), 1)

    def same(bs):
        sh = int(math.log2(bs))
        return lax.shift_right_logical(r, sh) == lax.shift_right_logical(c, sh)

    base = same(TRI_BASE)
    eye = jnp.where(r == c, 1.0, 0.0)
    ps = [jnp.where(base, a, 0.0) for a in mats]
    ts = [eye + p for p in ps]
    ps = [_mm(p, p) for p in ps]
    m = 2
    while 2 * m < TRI_BASE:
        both = [_mm(p, jnp.concatenate([p, t], axis=1)) for p, t in zip(ps, ts)]
        ts = [t + x[:, n:] for t, x in zip(ts, both)]
        ps = [x[:, :n] for x in both]
        m *= 2
    ts = [t + _mm(p, t) for p, t in zip(ps, ts)]
    bs = TRI_BASE
    while bs < blk:
        sel = same(2 * bs) & jnp.logical_not(same(bs))
        nblk = n // bs
        rows = lambda z, i: z[i * bs:(i + 1) * bs]
        t_odd = [jnp.concatenate([rows(t, i) for i in range(1, nblk, 2)], axis=0) for t in ts]
        lo = [_mm(to, jnp.where(sel, a, 0.0)) for to, a in zip(t_odd, mats)]
        upd = [_mm(l, t) for l, t in zip(lo, ts)]
        ts = [jnp.concatenate([rows(t, i) + rows(u, i // 2) if i % 2 else rows(t, i)
                               for i in range(nblk)], axis=0)
              for t, u in zip(ts, upd)]
        bs *= 2
    return ts


def _rwkv_scan_kernel(r_ref, k_ref, v_ref, lw_ref, kk_ref, b_ref, lg_ref, lb_ref, rk_ref,
                      y_ref, st_ref, *, chunk):
    @pl.when(pl.program_id(2) == 0)
    def _():
        st_ref[...] = jnp.zeros_like(st_ref)

    L = chunk
    n2 = 2 * L
    tb, width = r_ref.shape[1], r_ref.shape[2]
    chunks = range(tb // L)
    pairs = range(width // LANES)
    probs = [(p, ch) for ch in chunks for p in pairs]
    sh = int(math.log2(L))
    lane = lax.broadcasted_iota(jnp.int32, (L, LANES), 1)
    head0 = lane < RWKV_HEAD
    rr = lax.broadcasted_iota(jnp.int32, (n2, n2), 0)
    cc = lax.broadcasted_iota(jnp.int32, (n2, n2), 1)
    rt, ct = rr & (L - 1), cc & (L - 1)
    strict, incl, eye = rt > ct, rt >= ct, rr == cc
    tr = lax.broadcasted_iota(jnp.int32, (tb, tb), 0)
    tc = lax.broadcasted_iota(jnp.int32, (tb, tb), 1)
    same_chunk = lax.shift_right_logical(tr, sh) == lax.shift_right_logical(tc, sh)
    head0_tb = lax.broadcasted_iota(jnp.int32, (tb, LANES), 1) < RWKV_HEAD

    def head_sum(z):
        out = []
        for p in pairs:
            zp = z[:, p * LANES:(p + 1) * LANES]
            s0 = jnp.sum(jnp.where(head0_tb, zp, 0.0), axis=-1, keepdims=True)
            s1 = jnp.sum(jnp.where(head0_tb, 0.0, zp), axis=-1, keepdims=True)
            out.append(jnp.where(head0_tb, s0, s1))
        return jnp.concatenate(out, axis=1)

    r, k, v = r_ref[0].astype(F32), k_ref[0].astype(F32), v_ref[0].astype(F32)
    lw, kk = lw_ref[0], kk_ref[0].astype(F32)
    inv_norm = 1.0 / jnp.maximum(jnp.sqrt(head_sum(kk * kk)), 1e-12)
    kn = kk * inv_norm
    b = b_ref[0].astype(F32) * inv_norm
    tri = jnp.where(same_chunk & (tr >= tc), 1.0, 0.0).astype(BF16)
    c = _mm_exact_lhs(tri, lw)
    c_end = jnp.concatenate(
        [jnp.broadcast_to(c[(ch + 1) * L - 1:(ch + 1) * L, :], (L, width)) for ch in chunks], axis=0)
    e_inv = jnp.exp(-c)
    e_end = jnp.exp(c_end - c)
    p_end = jnp.exp(c_end)
    a_t = -kn * jnp.exp(c - lw)
    r_t = r * jnp.exp(c)
    bh, kh = b * e_inv, k * e_inv
    bb, kb = b * e_end, k * e_end

    def stack(z, prob):
        p, ch = prob
        z = z[ch * L:(ch + 1) * L, p * LANES:(p + 1) * LANES]
        return jnp.concatenate([jnp.where(head0, z, 0.0), jnp.where(head0, 0.0, z)], axis=0)

    a_s = [stack(a_t, q) for q in probs]
    r_s = [stack(r_t, q) for q in probs]
    v_s = [stack(v, q) for q in probs]
    ms = [_mm_nt(jnp.concatenate([a, rs], axis=0),
                 jnp.concatenate([stack(bh, q), stack(kh, q)], axis=0))
          for a, rs, q in zip(a_s, r_s, probs)]
    ts = _tri_inv([jnp.where(strict, m[:n2, :n2], 0.0) for m in ms], L)
    av = [_mm(jnp.where(strict, m[:n2, n2:], 0.0), vs) for m, vs in zip(ms, v_s)]
    tu = [_mm(t, jnp.concatenate([a, x], axis=1)) for t, a, x in zip(ts, a_s, av)]
    zeros = jnp.zeros((n2, LANES), F32)
    incl2 = jnp.concatenate([incl, incl], axis=1)
    big = [_mm(jnp.concatenate([jnp.where(incl2, m[n2:], 0.0),
                                jnp.concatenate([stack(bb, q), stack(kb, q)], axis=0).T], axis=0),
               jnp.concatenate([x, jnp.concatenate([zeros, vs], axis=1)], axis=0))
           for m, q, x, vs in zip(ms, probs, tu, v_s)]
    rg_lhs, y0, h0 = {}, {}, {}
    for i, (p, ch) in enumerate(probs):
        p_last = p_end[ch * L:ch * L + 1, p * LANES:(p + 1) * LANES]
        rg_lhs[p, ch] = big[i][:, :LANES] + jnp.concatenate(
            [r_s[i], jnp.where(eye, p_last, 0.0)], axis=0)
        y0[p, ch] = big[i][:n2, LANES:]
        h0[p, ch] = big[i][n2:, LANES:]

    st = [st_ref[p] for p in pairs]
    ys = [[] for _ in pairs]
    for ch in chunks:
        for p in pairs:
            rg = _mm(rg_lhs[p, ch], st[p])
            y2 = rg[:n2] + y0[p, ch]
            st[p] = rg[n2:] + h0[p, ch]
            ys[p].append(y2[:L] + y2[L:])
    for p in pairs:
        st_ref[p] = st[p]

    y = jnp.concatenate([jnp.concatenate(yp, axis=0) for yp in ys], axis=1)
    inv_n = 1.0 / RWKV_HEAD
    yc = y - head_sum(y) * inv_n
    var = head_sum(yc * yc) * inv_n
    yn = yc * lax.rsqrt(var + RWKV_GN_EPS) * lg_ref[...] + lb_ref[...]
    y_ref[0] = yn + head_sum(r * k * rk_ref[...]) * v


def _rwkv_post_kernel(y_ref, gate_ref, x_ref, wo_ref, g_ref, b_ref, o_ref, *, alpha):
    hout = _mm(y_ref[...] * gate_ref[...], wo_ref[...])
    o_ref[...] = _postnorm(alpha, x_ref[...], hout, g_ref[...], b_ref[...])


def _rwkv_mixer(x, layer, mu, w_rkv, w0, w1, w2, a0, a1, a2, g1, g2, k_k, k_a, r_k, lnx_g, lnx_b,
                w_out, g, b, alpha, tm=512, tblk=128):
    bsz, s, d = x.shape
    tm = min(tm, s)
    tile = pl.BlockSpec((1, tm, d), lambda bi, i: (bi, i, 0))
    hb = tm // SUBLANES
    outs = pl.pallas_call(
        _rwkv_prep_kernel,
        grid=(bsz, s // tm),
        in_specs=[tile,
                  pl.BlockSpec((1, SUBLANES, d), lambda bi, i: (bi, jnp.maximum(i * hb - 1, 0), 0)),
                  _full((6, d)), _layer(w_rkv, layer), _full((1, d)),
                  _layer(w1, layer), _layer(w2, layer), _full((1, d)),
                  _layer(a1, layer), _layer(a2, layer),
                  _layer(g1, layer), _layer(g2, layer),
                  _full((1, d)), _full((1, d))],
        out_specs=[tile] * 7,
        out_shape=[jax.ShapeDtypeStruct((bsz, s, d), F32 if name == "lw" else MXU_DTYPE)
                   for name in ("r", "k", "v", "lw", "kk", "b", "gate")],
        compiler_params=_cparams("parallel", "parallel"),
        name="rwkv_prep",
    )(x, x, mu, w_rkv, _row(w0), w1, w2, _row(a0), a1, a2, g1, g2, _row(k_k), _row(k_a))
    r, k, v, lw, kk, bvec, gate = outs

    tblk = min(tblk, s)
    width = SCAN_PAIRS * LANES
    blk = pl.BlockSpec((1, tblk, width), lambda bi, hp, i: (bi, i, hp))
    chan = pl.BlockSpec((1, width), lambda bi, hp, i: (0, hp))
    y = pl.pallas_call(
        functools.partial(_rwkv_scan_kernel, chunk=RWKV_CHUNK),
        grid=(bsz, d // width, s // tblk),
        in_specs=[blk] * 6 + [chan] * 3,
        out_specs=blk,
        out_shape=jax.ShapeDtypeStruct((bsz, s, d), F32),
        scratch_shapes=[pltpu.VMEM((SCAN_PAIRS, LANES, LANES), F32)],
        compiler_params=_cparams("parallel", "parallel", "arbitrary"),
        name="rwkv_scan",
    )(r, k, v, lw, kk, bvec, _row(lnx_g), _row(lnx_b), _row(r_k))

    n = bsz * s
    tm2 = min(tm, n)
    row = pl.BlockSpec((tm2, d), lambda i: (i, 0))
    flat = lambda z: z.reshape(n, d)
    out = pl.pallas_call(
        functools.partial(_rwkv_post_kernel, alpha=alpha),
        grid=(n // tm2,),
        in_specs=[row] * 3 + [_layer(w_out, layer), _full((1, d)), _full((1, d))],
        out_specs=row,
        out_shape=jax.ShapeDtypeStruct((n, d), F32),
        compiler_params=_cparams("parallel"),
        name="rwkv_post",
    )(flat(y), flat(gate), flat(x), w_out, _row(g), _row(b))
    return out.reshape(bsz, s, d)


def kernel(x, mem, rel_bias, a_w_in, a_b_in, a_dw, a_dw_b, a_ln_g, a_ln_b, a_w_out, a_b_out, b_w_qkv, b_w_out, c_mu, c_w_rkv, c_w0, c_w1, c_w2, c_a0, c_a1, c_a2, c_g1, c_g2, c_k_k, c_k_a, c_r_k, c_lnx_g, c_lnx_b, c_w_out, x_w_q, x_w_kv, x_w_out, m_w1, m_w2, ln_g, ln_b):
    depth = ln_g.shape[0]
    alpha = (2 * depth) ** 0.25
    bsz, s, d = x.shape
    mlen = mem.shape[1]
    mem2d = mem.reshape(bsz * mlen, d)
    for i in range(depth):
        kind, j = i % 3, i // 3
        if kind == 0:
            x = _conv_mixer(x, a_w_in, j, a_b_in[j], a_dw[j], a_dw_b[j], a_ln_g[j], a_ln_b[j],
                            a_w_out, a_b_out[j], ln_g[i, 0], ln_b[i, 0], alpha)
        elif kind == 1:
            x = _dil_mixer(x, b_w_qkv, b_w_out, j, rel_bias, ln_g[i, 0], ln_b[i, 0], alpha)
        else:
            x = _rwkv_mixer(x, j, c_mu[j], c_w_rkv, c_w0[j], c_w1, c_w2, c_a0[j],
                            c_a1, c_a2, c_g1, c_g2, c_k_k[j], c_k_a[j],
                            c_r_k[j], c_lnx_g[j], c_lnx_b[j], c_w_out,
                            ln_g[i, 0], ln_b[i, 0], alpha)
        kv = _linear(mem2d, x_w_kv, i, MXU_DTYPE, 512, 1024).reshape(bsz, mlen, 2 * d)
        x = _cross_attention(x, kv, x_w_q, x_w_out, i, ln_g[i, 1], ln_b[i, 1], alpha)
        x = _mlp(x.reshape(bsz * s, d), m_w1, m_w2, i, ln_g[i, 2], ln_b[i, 2],
                 alpha).reshape(bsz, s, d)
    return x
```
